```python
import math
import jax, jax.numpy as jnp
from jax import lax
import numpy as np

D_MODEL = 1024
BATCH = 1
SEQ = 16384
DEPTH = 1

CHUNK = 64
Q_BLOCK = 128
HEAD_DIM = 64
N_HEADS_FOX = 8
WIDTH_FOX = N_HEADS_FOX * HEAD_DIM
N_HEADS_DIFF = 4
DIFF_V_DIM = 2 * HEAD_DIM
WIDTH_DIFF = N_HEADS_DIFF * DIFF_V_DIM
WIDTH_DIFF_QK = 2 * N_HEADS_DIFF * HEAD_DIM
MIX_WIDTH = WIDTH_FOX + WIDTH_DIFF
COL_SIZES = (WIDTH_FOX, WIDTH_FOX, WIDTH_FOX, WIDTH_FOX, N_HEADS_FOX,
             WIDTH_DIFF_QK, WIDTH_DIFF_QK, WIDTH_DIFF, WIDTH_DIFF)
IN_COLS = 4 * WIDTH_FOX + N_HEADS_FOX + 2 * WIDTH_DIFF_QK + 2 * WIDTH_DIFF
ROPE_THETA = 10000.0
EPS = 1e-6
FORGET_BIAS_OFFSET = 3.0

kernel_name = "hymba_fox_diffattn_streaming_block"


def rms_norm(x, g):
    xf = x.astype(jnp.float32)
    y = xf * lax.rsqrt(jnp.mean(xf * xf, axis=-1, keepdims=True) + EPS)
    return (y * g.astype(jnp.float32)).astype(x.dtype)


def split_columns(h):
    outs, start = [], 0
    for size in COL_SIZES:
        outs.append(h[..., start:start + size])
        start += size
    return outs


def to_heads(t, n_heads):
    b, s, w = t.shape
    return t.reshape(b, s, n_heads, w // n_heads).transpose(0, 2, 1, 3)


def from_heads(t):
    b, h, s, d = t.shape
    return t.transpose(0, 2, 1, 3).reshape(b, s, h * d)


def rope(x, pos):
    d = x.shape[-1]
    inv_freq = ROPE_THETA ** (-jnp.arange(0, d, 2, dtype=jnp.float32) / d)
    ang = pos.astype(jnp.float32)[:, None] * inv_freq[None, :]
    cos, sin = jnp.cos(ang), jnp.sin(ang)
    xf = x.astype(jnp.float32)
    x1, x2 = xf[..., : d // 2], xf[..., d // 2:]
    out = jnp.concatenate([x1 * cos - x2 * sin, x2 * cos + x1 * sin], axis=-1)
    return out.astype(x.dtype)


def blocks_to_seq(o):
    nb, b, h, qb, d = o.shape
    return o.transpose(1, 2, 0, 3, 4).reshape(b, h, nb * qb, d)


def forgetting_attention(q, k, v, log_f):
    seq = q.shape[2]
    scale = HEAD_DIM ** -0.5
    cum_f = jnp.cumsum(log_f, axis=-1)
    kpos = jnp.arange(seq)

    def one_block(i):
        start = i * Q_BLOCK
        qb = lax.dynamic_slice_in_dim(q, start, Q_BLOCK, axis=2)
        fq = lax.dynamic_slice_in_dim(cum_f, start, Q_BLOCK, axis=2)
        s = jnp.einsum('bhqd,bhkd->bhqk', qb, k, preferred_element_type=jnp.float32) * scale
        s = s + (fq[..., :, None] - cum_f[..., None, :])
        qpos = start + jnp.arange(Q_BLOCK)
        mask = kpos[None, :] <= qpos[:, None]
        s = jnp.where(mask, s, -jnp.inf)
        p = jax.nn.softmax(s, axis=-1)
        return jnp.einsum('bhqk,bhkd->bhqd', p.astype(v.dtype), v)

    out = lax.map(one_block, jnp.arange(seq // Q_BLOCK))
    return blocks_to_seq(out)


def differential_attention(q, k, v, lam, subln_g, lambda_init):
    b, h2, seq, _ = q.shape
    scale = HEAD_DIM ** -0.5
    kchunk = jnp.arange(seq) // CHUNK

    def one_block(i):
        start = i * Q_BLOCK
        qb = lax.dynamic_slice_in_dim(q, start, Q_BLOCK, axis=2)
        s = jnp.einsum('bhqd,bhkd->bhqk', qb, k, preferred_element_type=jnp.float32) * scale
        qchunk = (start + jnp.arange(Q_BLOCK)) // CHUNK
        mask = kchunk[None, :] <= qchunk[:, None]
        s = jnp.where(mask, s, -jnp.inf)
        p = jax.nn.softmax(s, axis=-1).reshape(b, h2 // 2, 2, Q_BLOCK, seq)
        a = p[:, :, 0] - lam * p[:, :, 1]
        return jnp.einsum('bhqk,bhkd->bhqd', a.astype(v.dtype), v)

    out = blocks_to_seq(lax.map(one_block, jnp.arange(seq // Q_BLOCK)))
    out = rms_norm(out, subln_g)
    return (out.astype(jnp.float32) * (1.0 - lambda_init)).astype(v.dtype)


def setup_inputs(seed: int = 0) -> dict:
    key = jax.random.key(seed)
    ks = jax.random.split(key, 12)
    x = jax.random.normal(ks[0], (BATCH, SEQ, D_MODEL), jnp.float32)
    norm_g = 1.0 + 0.02 * jax.random.normal(ks[1], (DEPTH, D_MODEL), jnp.float32)
    w_in = jax.random.normal(ks[2], (DEPTH, D_MODEL, IN_COLS), jnp.float32) * D_MODEL ** -0.5
    b_forget = FORGET_BIAS_OFFSET + 0.5 * jax.random.normal(ks[3], (DEPTH, N_HEADS_FOX), jnp.float32)
    lambda_q1 = 0.1 * jax.random.normal(ks[4], (DEPTH, HEAD_DIM), jnp.float32)
    lambda_k1 = 0.1 * jax.random.normal(ks[5], (DEPTH, HEAD_DIM), jnp.float32)
    lambda_q2 = 0.1 * jax.random.normal(ks[6], (DEPTH, HEAD_DIM), jnp.float32)
    lambda_k2 = 0.1 * jax.random.normal(ks[7], (DEPTH, HEAD_DIM), jnp.float32)
    subln_g = 1.0 + 0.02 * jax.random.normal(ks[8], (DEPTH, DIFF_V_DIM), jnp.float32)
    w_out = jax.random.normal(ks[9], (DEPTH, MIX_WIDTH, D_MODEL), jnp.float32) * MIX_WIDTH ** -0.5
    final_g = 1.0 + 0.02 * jax.random.normal(ks[10], (D_MODEL,), jnp.float32)
    return {"x": x, "norm_g": norm_g, "w_in": w_in, "b_forget": b_forget,
            "lambda_q1": lambda_q1, "lambda_k1": lambda_k1,
            "lambda_q2": lambda_q2, "lambda_k2": lambda_k2,
            "subln_g": subln_g, "w_out": w_out, "final_g": final_g}


def reference(x, norm_g, w_in, b_forget, lambda_q1, lambda_k1, lambda_q2, lambda_k2,
              subln_g, w_out, final_g):
    seq = x.shape[1]
    pos = jnp.arange(seq, dtype=jnp.int32)
    h = x
    for layer in range(DEPTH):
        lambda_init = 0.8 - 0.6 * math.exp(-0.3 * layer)
        u = rms_norm(h, norm_g[layer])
        proj = jnp.einsum('bsd,dc->bsc', u, w_in[layer])
        (fq, fk, fv, fg, fz, dq, dk, dv, dg) = split_columns(proj)

        z = fz.astype(jnp.float32) + b_forget[layer].astype(jnp.float32)
        log_f = jax.nn.log_sigmoid(z).transpose(0, 2, 1)
        y_fox = forgetting_attention(to_heads(fq, N_HEADS_FOX), to_heads(fk, N_HEADS_FOX),
                                     to_heads(fv, N_HEADS_FOX), log_f)
        y_fox = from_heads(y_fox) * jax.nn.silu(fg)

        lam = (jnp.exp(jnp.sum(lambda_q1[layer].astype(jnp.float32) * lambda_k1[layer].astype(jnp.float32)))
               - jnp.exp(jnp.sum(lambda_q2[layer].astype(jnp.float32) * lambda_k2[layer].astype(jnp.float32)))
               + lambda_init)
        qd = rope(to_heads(dq, 2 * N_HEADS_DIFF), pos)
        kd = rope(to_heads(dk, 2 * N_HEADS_DIFF), pos)
        vd = to_heads(dv, N_HEADS_DIFF)
        y_diff = differential_attention(qd, kd, vd, lam, subln_g[layer], lambda_init)
        y_diff = from_heads(y_diff) * jax.nn.silu(dg)

        mixed = jnp.concatenate([y_fox, y_diff], axis=-1)
        h = h + jnp.einsum('bsc,cd->bsd', mixed, w_out[layer])
    return rms_norm(h, final_g)
```

```python
import functools
import math

import jax
import jax.numpy as jnp
import numpy as np
from jax import lax
from jax.experimental import pallas as pl
from jax.experimental.pallas import tpu as pltpu

D_MODEL = 1024
HEAD_DIM = 64
N_HEADS_FOX = 8
N_HEADS_DIFF = 4
N_MAPS_DIFF = 2 * N_HEADS_DIFF
DIFF_V_DIM = 2 * HEAD_DIM
WIDTH = 512
CHUNK = 64
ROPE_THETA = 10000.0
EPS = 1e-6
LAMBDA_INIT = 0.8 - 0.6 * math.exp(-0.3 * 0)
SCALE = HEAD_DIM ** -0.5

LANES = 128
ROW_BLOCK = 512
ATTN_BLOCK = 512
N_BIAS_PIECES = 3
VMEM_LIMIT_BYTES = 56 * 1024 * 1024

F32 = jnp.float32
BF16 = jnp.bfloat16

_T_GROUPS = ("fq", "fv", "fg", "dq", "dk", "dv", "dg")
_T_OFF = {name: i * WIDTH for i, name in enumerate(_T_GROUPS)}


def _dot_nt(a, b):
    return lax.dot_general(a, b, (((1,), (1,)), ((), ())), preferred_element_type=F32)


def _dot(a, b):
    return jnp.dot(a, b, preferred_element_type=F32)


def _split_bf16(x):
    hi = x.astype(BF16)
    r = x - hi.astype(F32)
    mid = r.astype(BF16)
    lo = (r - mid.astype(F32)).astype(BF16)
    return hi, mid, lo


def _proj_kernel(x_ref, ng_ref, wt_ref, wfk_ref, wfz_ref, bf_ref, invf_ref, sel_ref,
                 fqT_ref, fk_ref, fvT_ref, fgT_ref, dqT_ref, dk_ref, dvT_ref, dgT_ref,
                 carry_ref):
    i = pl.program_id(0)
    tm = x_ref.shape[0]

    @pl.when(i == 0)
    def _():
        carry_ref[...] = jnp.zeros_like(carry_ref)

    x = x_ref[...]
    u = x * lax.rsqrt(jnp.mean(x * x, axis=-1, keepdims=True) + EPS) * ng_ref[...]
    ub = u.astype(BF16)

    def proj_t(name, width=WIDTH):
        off = _T_OFF[name]
        return _dot_nt(wt_ref[off:off + width, :], ub)

    qT = proj_t("fq") * SCALE
    row = lax.broadcasted_iota(jnp.int32, (LANES - HEAD_DIM, tm), 0)
    ones_rows = jnp.where(row < N_BIAS_PIECES, 1.0, 0.0).astype(BF16)
    for h in range(N_HEADS_FOX):
        fqT_ref[h * LANES:h * LANES + HEAD_DIM, :] = qT[h * HEAD_DIM:(h + 1) * HEAD_DIM, :].astype(BF16)
        fqT_ref[h * LANES + HEAD_DIM:(h + 1) * LANES, :] = ones_rows
    fvT_ref[0] = proj_t("fv").astype(BF16)
    fg = proj_t("fg")
    fgT_ref[...] = (fg * jax.nn.sigmoid(fg)).astype(BF16)

    z = _dot(ub, wfz_ref[...]) + bf_ref[...]
    logf = jax.nn.log_sigmoid(z)
    r = lax.broadcasted_iota(jnp.int32, (tm, tm), 0)
    c = lax.broadcasted_iota(jnp.int32, (tm, tm), 1)
    tri = jnp.where(r >= c, 1.0, 0.0).astype(BF16)
    hi, mid, lo = _split_bf16(logf)
    cum = (_dot(tri, hi) + _dot(tri, mid)) + _dot(tri, lo) + carry_ref[...]
    carry_ref[...] = cum[tm - 1:tm, :]
    nh, nm, nl = _split_bf16(-cum)
    bias = (_dot(nh, sel_ref[0]) + _dot(nm, sel_ref[1])) + _dot(nl, sel_ref[2])
    fk_ref[...] = (_dot(ub, wfk_ref[...]) + bias).astype(BF16)

    pos = (i * tm + lax.broadcasted_iota(jnp.int32, (1, tm), 1)).astype(F32)
    ang = pos * invf_ref[...]
    cos, sin = jnp.cos(ang), jnp.sin(ang)
    half = HEAD_DIM // 2

    def rope_t(xt):
        x1, x2 = xt[:half, :], xt[half:, :]
        return jnp.concatenate([x1 * cos - x2 * sin, x2 * cos + x1 * sin], axis=0)

    qT = proj_t("dq")
    zeros = jnp.zeros((HEAD_DIM, tm), BF16)
    for m in range(N_MAPS_DIFF):
        qm = (rope_t(qT[m * HEAD_DIM:(m + 1) * HEAD_DIM, :]) * SCALE).astype(BF16)
        lo_rows, hi_rows = (qm, zeros) if m % 2 == 0 else (zeros, qm)
        dqT_ref[m * LANES:m * LANES + HEAD_DIM, :] = lo_rows
        dqT_ref[m * LANES + HEAD_DIM:(m + 1) * LANES, :] = hi_rows
    kT = proj_t("dk")
    kT = jnp.concatenate([rope_t(kT[m * HEAD_DIM:(m + 1) * HEAD_DIM, :]) for m in range(N_MAPS_DIFF)], axis=0)
    dk_ref[...] = kT.T.astype(BF16)
    dvT_ref[0] = proj_t("dv").astype(BF16)
    dg = proj_t("dg")
    dgT_ref[...] = (dg * jax.nn.sigmoid(dg)).astype(BF16)


def _project(x2d, norm_g, wt, wfk, wfz, bf, invf, sel):
    seq = x2d.shape[0]
    tm = ROW_BLOCK
    nblk = seq // tm
    const = lambda shape: pl.BlockSpec(shape, lambda i: (0,) * len(shape))
    colT = lambda rows: pl.BlockSpec((rows, tm), lambda i: (0, i))
    chunkT = pl.BlockSpec((1, WIDTH, tm), lambda i: (i, 0, 0))
    out_shape = (
        jax.ShapeDtypeStruct((N_HEADS_FOX * LANES, seq), BF16),
        jax.ShapeDtypeStruct((seq, N_HEADS_FOX * LANES), BF16),
        jax.ShapeDtypeStruct((nblk, WIDTH, tm), BF16),
        jax.ShapeDtypeStruct((WIDTH, seq), BF16),
        jax.ShapeDtypeStruct((N_MAPS_DIFF * LANES, seq), BF16),
        jax.ShapeDtypeStruct((seq, WIDTH), BF16),
        jax.ShapeDtypeStruct((nblk, WIDTH, tm), BF16),
        jax.ShapeDtypeStruct((WIDTH, seq), BF16),
    )
    return pl.pallas_call(
        _proj_kernel,
        grid=(nblk,),
        in_specs=[
            pl.BlockSpec((tm, D_MODEL), lambda i: (i, 0)),
            const((1, D_MODEL)),
            const(wt.shape), const(wfk.shape), const(wfz.shape), const(bf.shape),
            const(invf.shape), const(sel.shape),
        ],
        out_specs=(
            colT(N_HEADS_FOX * LANES),
            pl.BlockSpec((tm, N_HEADS_FOX * LANES), lambda i: (i, 0)),
            chunkT, colT(WIDTH),
            colT(N_MAPS_DIFF * LANES),
            pl.BlockSpec((tm, WIDTH), lambda i: (i, 0)),
            chunkT, colT(WIDTH),
        ),
        out_shape=out_shape,
        scratch_shapes=[pltpu.VMEM((1, LANES), F32)],
        compiler_params=pltpu.CompilerParams(
            dimension_semantics=("arbitrary",), vmem_limit_bytes=VMEM_LIMIT_BYTES),
        name="proj",
    )(x2d, norm_g, wt, wfk, wfz, bf, invf, sel)


def _attn_sweep(rhs, k_ref, vT_ref, qi, blk, dv, mask_fn):
    nq = rhs.shape[1]

    def step(kb, carry, masked):
        m, l, acc = carry
        k = k_ref[pl.ds(pl.multiple_of(kb * blk, blk), blk), :]
        s = _dot(k, rhs)
        if masked:
            s = jnp.where(mask_fn(s.shape), s, -jnp.inf)
        m_new = jnp.maximum(m, jnp.max(s, axis=0, keepdims=True))
        alpha = jnp.exp(m - m_new)
        p = jnp.exp(s - m_new)
        l = alpha * l + jnp.sum(p, axis=0, keepdims=True)
        acc = alpha * acc + _dot(vT_ref[kb], p.astype(BF16))
        return m_new, l, acc

    init = (jnp.full((1, nq), -jnp.inf, F32), jnp.zeros((1, nq), F32), jnp.zeros((dv, nq), F32))
    carry = lax.fori_loop(0, qi, functools.partial(step, masked=False), init)
    _, l, acc = step(qi, carry, masked=True)
    return acc, l


def _fox_kernel(qT_ref, k_ref, vT_ref, gT_ref, o_ref):
    qi = pl.program_id(1)
    blk = qT_ref.shape[1]

    def mask_fn(shape):
        return lax.broadcasted_iota(jnp.int32, shape, 0) <= lax.broadcasted_iota(jnp.int32, shape, 1)

    acc, l = _attn_sweep(qT_ref[...], k_ref, vT_ref, qi, blk, HEAD_DIM, mask_fn)
    o_ref[...] = ((acc / l) * gT_ref[...].astype(F32)).astype(o_ref.dtype)


def _diff_kernel(q1T_ref, q2T_ref, k_ref, vT_ref, gT_ref, lq1_ref, lk1_ref, lq2_ref, lk2_ref, sg_ref, o_ref):
    qi = pl.program_id(1)
    blk = q1T_ref.shape[1]

    def mask_fn(shape):
        kc = lax.broadcasted_iota(jnp.int32, shape, 0) // CHUNK
        qc = (lax.broadcasted_iota(jnp.int32, shape, 1) % blk) // CHUNK
        return kc <= qc

    rhs = jnp.concatenate([q1T_ref[...], q2T_ref[...]], axis=1)
    acc, l = _attn_sweep(rhs, k_ref, vT_ref, qi, blk, DIFF_V_DIM, mask_fn)
    o = acc / l
    lam = (jnp.exp(jnp.sum(lq1_ref[...] * lk1_ref[...], axis=-1, keepdims=True))
           - jnp.exp(jnp.sum(lq2_ref[...] * lk2_ref[...], axis=-1, keepdims=True)) + LAMBDA_INIT)
    y = o[:, :blk] - lam * o[:, blk:]
    y = y * lax.rsqrt(jnp.mean(y * y, axis=0, keepdims=True) + EPS) * sg_ref[...]
    y = y * (1.0 - LAMBDA_INIT)
    o_ref[...] = (y * gT_ref[...].astype(F32)).astype(o_ref.dtype)


def _fox_attention(fqT, fk, fvT, fgT):
    seq = fk.shape[0]
    blk = ATTN_BLOCK
    nblk = seq // blk
    return pl.pallas_call(
        _fox_kernel,
        grid=(N_HEADS_FOX, nblk),
        in_specs=[
            pl.BlockSpec((LANES, blk), lambda h, i: (h, i)),
            pl.BlockSpec((seq, LANES), lambda h, i: (0, h)),
            pl.BlockSpec((nblk, HEAD_DIM, blk), lambda h, i: (0, h, 0)),
            pl.BlockSpec((HEAD_DIM, blk), lambda h, i: (h, i)),
        ],
        out_specs=pl.BlockSpec((HEAD_DIM, blk), lambda h, i: (h, i)),
        out_shape=jax.ShapeDtypeStruct((WIDTH, seq), BF16),
        compiler_params=pltpu.CompilerParams(
            dimension_semantics=("arbitrary", "arbitrary"), vmem_limit_bytes=VMEM_LIMIT_BYTES),
        name="fox_attn",
    )(fqT, fk, fvT, fgT)


def _diff_attention(dqT, dk, dvT, dgT, lq1, lk1, lq2, lk2, sg):
    seq = dk.shape[0]
    blk = ATTN_BLOCK
    nblk = seq // blk
    small = lambda a: pl.BlockSpec(a.shape, lambda h, i: (0, 0))
    return pl.pallas_call(
        _diff_kernel,
        grid=(N_HEADS_DIFF, nblk),
        in_specs=[
            pl.BlockSpec((LANES, blk), lambda h, i: (2 * h, i)),
            pl.BlockSpec((LANES, blk), lambda h, i: (2 * h + 1, i)),
            pl.BlockSpec((seq, LANES), lambda h, i: (0, h)),
            pl.BlockSpec((nblk, DIFF_V_DIM, blk), lambda h, i: (0, h, 0)),
            pl.BlockSpec((DIFF_V_DIM, blk), lambda h, i: (h, i)),
            small(lq1), small(lk1), small(lq2), small(lk2), small(sg),
        ],
        out_specs=pl.BlockSpec((DIFF_V_DIM, blk), lambda h, i: (h, i)),
        out_shape=jax.ShapeDtypeStruct((WIDTH, seq), BF16),
        compiler_params=pltpu.CompilerParams(
            dimension_semantics=("arbitrary", "arbitrary"), vmem_limit_bytes=VMEM_LIMIT_BYTES),
        name="diff_attn",
    )(dqT, dqT, dk, dvT, dgT, lq1, lk1, lq2, lk2, sg)


def _out_kernel(mf_ref, md_ref, x_ref, w_ref, g_ref, o_ref):
    tn = (((0,), (0,)), ((), ()))
    y = lax.dot_general(mf_ref[...], w_ref[:WIDTH, :], tn, preferred_element_type=F32)
    y = y + lax.dot_general(md_ref[...], w_ref[WIDTH:, :], tn, preferred_element_type=F32)
    h = x_ref[...] + y
    o_ref[...] = h * lax.rsqrt(jnp.mean(h * h, axis=-1, keepdims=True) + EPS) * g_ref[...]


def _out_project(mixT_fox, mixT_diff, x2d, w_out, final_g):
    seq = x2d.shape[0]
    tm = ROW_BLOCK
    return pl.pallas_call(
        _out_kernel,
        grid=(seq // tm,),
        in_specs=[
            pl.BlockSpec((WIDTH, tm), lambda i: (0, i)),
            pl.BlockSpec((WIDTH, tm), lambda i: (0, i)),
            pl.BlockSpec((tm, D_MODEL), lambda i: (i, 0)),
            pl.BlockSpec(w_out.shape, lambda i: (0, 0)),
            pl.BlockSpec((1, D_MODEL), lambda i: (0, 0)),
        ],
        out_specs=pl.BlockSpec((tm, D_MODEL), lambda i: (i, 0)),
        out_shape=jax.ShapeDtypeStruct((seq, D_MODEL), F32),
        compiler_params=pltpu.CompilerParams(
            dimension_semantics=("arbitrary",), vmem_limit_bytes=VMEM_LIMIT_BYTES),
        name="out_proj",
    )(mixT_fox, mixT_diff, x2d, w_out, final_g)


def _bias_selector():
    sel = np.zeros((N_BIAS_PIECES, LANES, N_HEADS_FOX * LANES), np.float32)
    for p in range(N_BIAS_PIECES):
        for h in range(N_HEADS_FOX):
            sel[p, h, h * LANES + HEAD_DIM + p] = 1.0
    return jnp.asarray(sel, BF16)


def kernel(x, norm_g, w_in, b_forget, lambda_q1, lambda_k1, lambda_q2, lambda_k2, subln_g, w_out, final_g):
    batch, seq, _ = x.shape
    assert batch == 1 and seq % ATTN_BLOCK == 0 and ATTN_BLOCK == ROW_BLOCK
    x2d = x[0]
    w = w_in[0]
    sizes = (WIDTH, WIDTH, WIDTH, WIDTH, N_HEADS_FOX, WIDTH, WIDTH, WIDTH, WIDTH)
    offs = np.concatenate([[0], np.cumsum(sizes)])
    g = {n: w[:, offs[j]:offs[j + 1]] for j, n in enumerate(("fq", "fk", "fv", "fg", "fz", "dq", "dk", "dv", "dg"))}
    wt = jnp.concatenate([g[n] for n in _T_GROUPS], axis=1).T.astype(BF16)
    wfk = jnp.pad(g["fk"].reshape(D_MODEL, N_HEADS_FOX, HEAD_DIM),
                  ((0, 0), (0, 0), (0, LANES - HEAD_DIM))).reshape(D_MODEL, N_HEADS_FOX * LANES).astype(BF16)
    wfz = jnp.pad(g["fz"], ((0, 0), (0, LANES - N_HEADS_FOX))).astype(BF16)
    bf = jnp.pad(b_forget[0].astype(F32), (0, LANES - N_HEADS_FOX)).reshape(1, LANES)
    invf = (ROPE_THETA ** (-jnp.arange(0, HEAD_DIM, 2, dtype=F32) / HEAD_DIM)).reshape(HEAD_DIM // 2, 1)

    fqT, fk, fvT, fgT, dqT, dk, dvT, dgT = _project(
        x2d, norm_g[0].reshape(1, D_MODEL), wt, wfk, wfz, bf, invf, _bias_selector())

    mixT_fox = _fox_attention(fqT, fk, fvT, fgT)
    row = lambda a: a[0].astype(F32).reshape(1, HEAD_DIM)
    mixT_diff = _diff_attention(dqT, dk, dvT, dgT, row(lambda_q1), row(lambda_k1), row(lambda_q2), row(lambda_k2),
                                subln_g[0].astype(F32).reshape(DIFF_V_DIM, 1))
    out = _out_project(mixT_fox, mixT_diff, x2d, w_out[0].astype(BF16), final_g.reshape(1, D_MODEL))
    return out[None]
```

```python
import functools
import math

import jax
import jax.numpy as jnp
import numpy as np
from jax import lax
from jax.experimental import pallas as pl
from jax.experimental.pallas import tpu as pltpu

D_MODEL = 1024
HEAD_DIM = 64
N_HEADS_FOX = 8
N_HEADS_DIFF = 4
N_MAPS_DIFF = 2 * N_HEADS_DIFF
DIFF_V_DIM = 2 * HEAD_DIM
WIDTH = 512
CHUNK = 64
ROPE_THETA = 10000.0
EPS = 1e-6
LAMBDA_INIT = 0.8 - 0.6 * math.exp(-0.3 * 0)
SCALE = HEAD_DIM ** -0.5
LOG2E = math.log2(math.e)
Q_SCALE = SCALE * LOG2E

LANES = 128
ROW_BLOCK = 512
ATTN_BLOCK = 1024
N_BIAS_PIECES = 3
BF16_SUBLANES = 16
FOX_V_ROWS = HEAD_DIM + BF16_SUBLANES
DIFF_V_ROWS = DIFF_V_DIM + BF16_SUBLANES
VMEM_LIMIT_BYTES = 56 * 1024 * 1024

F32 = jnp.float32
BF16 = jnp.bfloat16

_T_GROUPS = ("fq", "fv", "fg", "dq", "dk", "dv", "dg")
_T_OFF = {name: i * WIDTH for i, name in enumerate(_T_GROUPS)}


def _dot_nt(a, b):
    return lax.dot_general(a, b, (((1,), (1,)), ((), ())), preferred_element_type=F32)


def _dot(a, b):
    return jnp.dot(a, b, preferred_element_type=F32)


def _split_bf16(x):
    hi = x.astype(BF16)
    r = x - hi.astype(F32)
    mid = r.astype(BF16)
    lo = (r - mid.astype(F32)).astype(BF16)
    return hi, mid, lo


def _proj_kernel(x_ref, ng_ref, wt_ref, wfk_ref, wfz_ref, bf_ref, invf_ref, sel_ref,
                 fqT_ref, fk_ref, fvT_ref, fgT_ref, dqT_ref, dk_ref, dvT_ref, dgT_ref,
                 carry_ref):
    i = pl.program_id(0)
    tm = x_ref.shape[0]

    @pl.when(i == 0)
    def _():
        carry_ref[...] = jnp.zeros_like(carry_ref)

    x = x_ref[...]
    u = x * lax.rsqrt(jnp.mean(x * x, axis=-1, keepdims=True) + EPS) * ng_ref[...]
    ub = u.astype(BF16)

    def proj_t(name, width=WIDTH):
        off = _T_OFF[name]
        return _dot_nt(wt_ref[off:off + width, :], ub)

    qT = proj_t("fq") * Q_SCALE
    row = lax.broadcasted_iota(jnp.int32, (LANES - HEAD_DIM, tm), 0)
    ones_rows = jnp.where(row < N_BIAS_PIECES, 1.0, 0.0).astype(BF16)
    for h in range(N_HEADS_FOX):
        fqT_ref[h * LANES:h * LANES + HEAD_DIM, :] = qT[h * HEAD_DIM:(h + 1) * HEAD_DIM, :].astype(BF16)
        fqT_ref[h * LANES + HEAD_DIM:(h + 1) * LANES, :] = ones_rows
    row = lax.broadcasted_iota(jnp.int32, (BF16_SUBLANES, tm), 0)
    denom_rows = jnp.where(row == 0, 1.0, 0.0).astype(BF16)
    vT = proj_t("fv").astype(BF16)
    for h in range(N_HEADS_FOX):
        fvT_ref[0, h * FOX_V_ROWS:h * FOX_V_ROWS + HEAD_DIM, :] = vT[h * HEAD_DIM:(h + 1) * HEAD_DIM, :]
        fvT_ref[0, h * FOX_V_ROWS + HEAD_DIM:(h + 1) * FOX_V_ROWS, :] = denom_rows
    fg = proj_t("fg")
    fgT_ref[...] = (fg * jax.nn.sigmoid(fg)).astype(BF16)

    z = _dot(ub, wfz_ref[...]) + bf_ref[...]
    logf = jax.nn.log_sigmoid(z)
    r = lax.broadcasted_iota(jnp.int32, (tm, tm), 0)
    c = lax.broadcasted_iota(jnp.int32, (tm, tm), 1)
    tri = jnp.where(r >= c, 1.0, 0.0).astype(BF16)
    hi, mid, lo = _split_bf16(logf)
    cum = (_dot(tri, hi) + _dot(tri, mid)) + _dot(tri, lo) + carry_ref[...]
    carry_ref[...] = cum[tm - 1:tm, :]
    nh, nm, nl = _split_bf16(cum * (-LOG2E))
    bias = (_dot(nh, sel_ref[0]) + _dot(nm, sel_ref[1])) + _dot(nl, sel_ref[2])
    fk_ref[...] = (_dot(ub, wfk_ref[...]) + bias).astype(BF16)

    pos = (i * tm + lax.broadcasted_iota(jnp.int32, (1, tm), 1)).astype(F32)
    ang = pos * invf_ref[...]
    cos, sin = jnp.cos(ang), jnp.sin(ang)
    half = HEAD_DIM // 2

    def rope_t(xt):
        x1, x2 = xt[:half, :], xt[half:, :]
        return jnp.concatenate([x1 * cos - x2 * sin, x2 * cos + x1 * sin], axis=0)

    qT = proj_t("dq")
    zeros = jnp.zeros((HEAD_DIM, tm), BF16)
    for m in range(N_MAPS_DIFF):
        qm = (rope_t(qT[m * HEAD_DIM:(m + 1) * HEAD_DIM, :]) * Q_SCALE).astype(BF16)
        lo_rows, hi_rows = (qm, zeros) if m % 2 == 0 else (zeros, qm)
        dqT_ref[m * LANES:m * LANES + HEAD_DIM, :] = lo_rows
        dqT_ref[m * LANES + HEAD_DIM:(m + 1) * LANES, :] = hi_rows
    kT = proj_t("dk")
    kT = jnp.concatenate([rope_t(kT[m * HEAD_DIM:(m + 1) * HEAD_DIM, :]) for m in range(N_MAPS_DIFF)], axis=0)
    dk_ref[...] = kT.T.astype(BF16)
    vT = proj_t("dv").astype(BF16)
    for h in range(N_HEADS_DIFF):
        dvT_ref[0, h * DIFF_V_ROWS:h * DIFF_V_ROWS + DIFF_V_DIM, :] = vT[h * DIFF_V_DIM:(h + 1) * DIFF_V_DIM, :]
        dvT_ref[0, h * DIFF_V_ROWS + DIFF_V_DIM:(h + 1) * DIFF_V_ROWS, :] = denom_rows
    dg = proj_t("dg")
    dgT_ref[...] = (dg * jax.nn.sigmoid(dg)).astype(BF16)


def _project(x2d, norm_g, wt, wfk, wfz, bf, invf, sel):
    seq = x2d.shape[0]
    tm = ROW_BLOCK
    nblk = seq // tm
    const = lambda shape: pl.BlockSpec(shape, lambda i: (0,) * len(shape))
    colT = lambda rows: pl.BlockSpec((rows, tm), lambda i: (0, i))
    chunkT = lambda rows: pl.BlockSpec((1, rows, tm), lambda i: (i, 0, 0))
    out_shape = (
        jax.ShapeDtypeStruct((N_HEADS_FOX * LANES, seq), BF16),
        jax.ShapeDtypeStruct((seq, N_HEADS_FOX * LANES), BF16),
        jax.ShapeDtypeStruct((nblk, N_HEADS_FOX * FOX_V_ROWS, tm), BF16),
        jax.ShapeDtypeStruct((WIDTH, seq), BF16),
        jax.ShapeDtypeStruct((N_MAPS_DIFF * LANES, seq), BF16),
        jax.ShapeDtypeStruct((seq, WIDTH), BF16),
        jax.ShapeDtypeStruct((nblk, N_HEADS_DIFF * DIFF_V_ROWS, tm), BF16),
        jax.ShapeDtypeStruct((WIDTH, seq), BF16),
    )
    return pl.pallas_call(
        _proj_kernel,
        grid=(nblk,),
        in_specs=[
            pl.BlockSpec((tm, D_MODEL), lambda i: (i, 0)),
            const((1, D_MODEL)),
            const(wt.shape), const(wfk.shape), const(wfz.shape), const(bf.shape),
            const(invf.shape), const(sel.shape),
        ],
        out_specs=(
            colT(N_HEADS_FOX * LANES),
            pl.BlockSpec((tm, N_HEADS_FOX * LANES), lambda i: (i, 0)),
            chunkT(N_HEADS_FOX * FOX_V_ROWS), colT(WIDTH),
            colT(N_MAPS_DIFF * LANES),
            pl.BlockSpec((tm, WIDTH), lambda i: (i, 0)),
            chunkT(N_HEADS_DIFF * DIFF_V_ROWS), colT(WIDTH),
        ),
        out_shape=out_shape,
        scratch_shapes=[pltpu.VMEM((1, LANES), F32)],
        compiler_params=pltpu.CompilerParams(
            dimension_semantics=("arbitrary",), vmem_limit_bytes=VMEM_LIMIT_BYTES),
        name="proj",
    )(x2d, norm_g, wt, wfk, wfz, bf, invf, sel)


def _attn_sweep(rhs_ref, k_ref, vT_ref, sa_ref, sb_ref, qi, blk, mask_fn):
    nq = sa_ref.shape[1]
    v_rows, v_chunk = vT_ref.shape[1:]
    chunks = blk // v_chunk

    def block_of(t):
        return jnp.where(t == 0, qi, t - 1)

    def produce(t, s_ref, masked=False):
        kb = block_of(t)
        k = k_ref[pl.ds(pl.multiple_of(kb * blk, blk), blk), :]
        s = _dot(k, rhs_ref[...])
        if masked:
            s = jnp.where(mask_fn(s.shape), s, -jnp.inf)
        s_ref[...] = s
        return jnp.max(s, axis=0, keepdims=True)

    def consume(t, s_ref, m_blk, m, acc):
        m_new = jnp.maximum(m, m_blk)
        alpha = jnp.exp2(m - m_new)
        acc = alpha * acc
        for c in range(chunks):
            p = jnp.exp2(s_ref[c * v_chunk:(c + 1) * v_chunk, :] - m_new).astype(BF16)
            acc = acc + _dot(vT_ref[block_of(t) * chunks + c], p)
        return m_new, acc

    def pair(i, carry):
        ma, m, acc = carry
        t = 2 * i
        mb = produce(t + 1, sb_ref)
        m, acc = consume(t, sa_ref, ma, m, acc)
        ma = produce(t + 2, sa_ref)
        m, acc = consume(t + 1, sb_ref, mb, m, acc)
        return ma, m, acc

    init = (produce(0, sa_ref, masked=True), jnp.full((1, nq), -jnp.inf, F32), jnp.zeros((v_rows, nq), F32))
    n_pairs = qi // 2
    carry = lax.fori_loop(0, n_pairs, pair, init)
    t = 2 * n_pairs

    def one_left(carry):
        ma, m, acc = carry
        return consume(t, sa_ref, ma, m, acc)[1]

    def two_left(carry):
        ma, m, acc = carry
        mb = produce(t + 1, sb_ref)
        m, acc = consume(t, sa_ref, ma, m, acc)
        return consume(t + 1, sb_ref, mb, m, acc)[1]

    return lax.cond(qi % 2 == 1, two_left, one_left, carry)


def _fox_kernel(qT_ref, k_ref, vT_ref, gT_ref, o_ref, sa_ref, sb_ref):
    qi = pl.program_id(1)
    blk = qT_ref.shape[1]

    def mask_fn(shape):
        return lax.broadcasted_iota(jnp.int32, shape, 0) <= lax.broadcasted_iota(jnp.int32, shape, 1)

    acc = _attn_sweep(qT_ref, k_ref, vT_ref, sa_ref, sb_ref, qi, blk, mask_fn)
    o = acc[:HEAD_DIM, :] / acc[HEAD_DIM:HEAD_DIM + 1, :]
    o_ref[...] = (o * gT_ref[...].astype(F32)).astype(o_ref.dtype)


def _diff_kernel(q1T_ref, q2T_ref, k_ref, vT_ref, gT_ref, lq1_ref, lk1_ref, lq2_ref, lk2_ref, sg_ref, o_ref,
                 rhs_ref, sa_ref, sb_ref):
    qi = pl.program_id(1)
    blk = q1T_ref.shape[1]

    def mask_fn(shape):
        kc = lax.broadcasted_iota(jnp.int32, shape, 0) // CHUNK
        qc = (lax.broadcasted_iota(jnp.int32, shape, 1) % blk) // CHUNK
        return kc <= qc

    rhs_ref[:, :blk] = q1T_ref[...]
    rhs_ref[:, blk:] = q2T_ref[...]
    acc = _attn_sweep(rhs_ref, k_ref, vT_ref, sa_ref, sb_ref, qi, blk, mask_fn)
    o = acc[:DIFF_V_DIM, :] / acc[DIFF_V_DIM:DIFF_V_DIM + 1, :]
    lam = (jnp.exp(jnp.sum(lq1_ref[...] * lk1_ref[...], axis=-1, keepdims=True))
           - jnp.exp(jnp.sum(lq2_ref[...] * lk2_ref[...], axis=-1, keepdims=True)) + LAMBDA_INIT)
    y = o[:, :blk] - lam * o[:, blk:]
    y = y * lax.rsqrt(jnp.mean(y * y, axis=0, keepdims=True) + EPS) * sg_ref[...]
    y = y * (1.0 - LAMBDA_INIT)
    o_ref[...] = (y * gT_ref[...].astype(F32)).astype(o_ref.dtype)


def _fox_attention(fqT, fk, fvT, fgT):
    seq = fk.shape[0]
    blk = ATTN_BLOCK
    nblk = seq // blk
    return pl.pallas_call(
        _fox_kernel,
        grid=(N_HEADS_FOX, nblk),
        in_specs=[
            pl.BlockSpec((LANES, blk), lambda h, i: (h, i)),
            pl.BlockSpec((seq, LANES), lambda h, i: (0, h)),
            pl.BlockSpec((fvT.shape[0], FOX_V_ROWS, fvT.shape[2]), lambda h, i: (0, h, 0)),
            pl.BlockSpec((HEAD_DIM, blk), lambda h, i: (h, i)),
        ],
        out_specs=pl.BlockSpec((HEAD_DIM, blk), lambda h, i: (h, i)),
        out_shape=jax.ShapeDtypeStruct((WIDTH, seq), BF16),
        scratch_shapes=[pltpu.VMEM((blk, blk), F32), pltpu.VMEM((blk, blk), F32)],
        compiler_params=pltpu.CompilerParams(
            dimension_semantics=("arbitrary", "arbitrary"), vmem_limit_bytes=VMEM_LIMIT_BYTES),
        name="fox_attn",
    )(fqT, fk, fvT, fgT)


def _diff_attention(dqT, dk, dvT, dgT, lq1, lk1, lq2, lk2, sg):
    seq = dk.shape[0]
    blk = ATTN_BLOCK
    nblk = seq // blk
    small = lambda a: pl.BlockSpec(a.shape, lambda h, i: (0, 0))
    return pl.pallas_call(
        _diff_kernel,
        grid=(N_HEADS_DIFF, nblk),
        in_specs=[
            pl.BlockSpec((LANES, blk), lambda h, i: (2 * h, i)),
            pl.BlockSpec((LANES, blk), lambda h, i: (2 * h + 1, i)),
            pl.BlockSpec((seq, LANES), lambda h, i: (0, h)),
            pl.BlockSpec((dvT.shape[0], DIFF_V_ROWS, dvT.shape[2]), lambda h, i: (0, h, 0)),
            pl.BlockSpec((DIFF_V_DIM, blk), lambda h, i: (h, i)),
            small(lq1), small(lk1), small(lq2), small(lk2), small(sg),
        ],
        out_specs=pl.BlockSpec((DIFF_V_DIM, blk), lambda h, i: (h, i)),
        out_shape=jax.ShapeDtypeStruct((WIDTH, seq), BF16),
        scratch_shapes=[pltpu.VMEM((LANES, 2 * blk), BF16),
                        pltpu.VMEM((blk, 2 * blk), F32), pltpu.VMEM((blk, 2 * blk), F32)],
        compiler_params=pltpu.CompilerParams(
            dimension_semantics=("arbitrary", "arbitrary"), vmem_limit_bytes=VMEM_LIMIT_BYTES),
        name="diff_attn",
    )(dqT, dqT, dk, dvT, dgT, lq1, lk1, lq2, lk2, sg)


def _out_kernel(mf_ref, md_ref, x_ref, w_ref, g_ref, o_ref):
    tn = (((0,), (0,)), ((), ()))
    y = lax.dot_general(mf_ref[...], w_ref[:WIDTH, :], tn, preferred_element_type=F32)
    y = y + lax.dot_general(md_ref[...], w_ref[WIDTH:, :], tn, preferred_element_type=F32)
    h = x_ref[...] + y
    o_ref[...] = h * lax.rsqrt(jnp.mean(h * h, axis=-1, keepdims=True) + EPS) * g_ref[...]


def _out_project(mixT_fox, mixT_diff, x2d, w_out, final_g):
    seq = x2d.shape[0]
    tm = ROW_BLOCK
    return pl.pallas_call(
        _out_kernel,
        grid=(seq // tm,),
        in_specs=[
            pl.BlockSpec((WIDTH, tm), lambda i: (0, i)),
            pl.BlockSpec((WIDTH, tm), lambda i: (0, i)),
            pl.BlockSpec((tm, D_MODEL), lambda i: (i, 0)),
            pl.BlockSpec(w_out.shape, lambda i: (0, 0)),
            pl.BlockSpec((1, D_MODEL), lambda i: (0, 0)),
        ],
        out_specs=pl.BlockSpec((tm, D_MODEL), lambda i: (i, 0)),
        out_shape=jax.ShapeDtypeStruct((seq, D_MODEL), F32),
        compiler_params=pltpu.CompilerParams(
            dimension_semantics=("arbitrary",), vmem_limit_bytes=VMEM_LIMIT_BYTES),
        name="out_proj",
    )(mixT_fox, mixT_diff, x2d, w_out, final_g)


def _bias_selector():
    sel = np.zeros((N_BIAS_PIECES, LANES, N_HEADS_FOX * LANES), np.float32)
    for p in range(N_BIAS_PIECES):
        for h in range(N_HEADS_FOX):
            sel[p, h, h * LANES + HEAD_DIM + p] = 1.0
    return jnp.asarray(sel, BF16)


def kernel(x, norm_g, w_in, b_forget, lambda_q1, lambda_k1, lambda_q2, lambda_k2, subln_g, w_out, final_g):
    batch, seq, _ = x.shape
    assert batch == 1 and seq % ATTN_BLOCK == 0 and ATTN_BLOCK % ROW_BLOCK == 0
    x2d = x[0]
    w = w_in[0]
    sizes = (WIDTH, WIDTH, WIDTH, WIDTH, N_HEADS_FOX, WIDTH, WIDTH, WIDTH, WIDTH)
    offs = np.concatenate([[0], np.cumsum(sizes)])
    g = {n: w[:, offs[j]:offs[j + 1]] for j, n in enumerate(("fq", "fk", "fv", "fg", "fz", "dq", "dk", "dv", "dg"))}
    wt = jnp.concatenate([g[n] for n in _T_GROUPS], axis=1).T.astype(BF16)
    wfk = jnp.pad(g["fk"].reshape(D_MODEL, N_HEADS_FOX, HEAD_DIM),
                  ((0, 0), (0, 0), (0, LANES - HEAD_DIM))).reshape(D_MODEL, N_HEADS_FOX * LANES).astype(BF16)
    wfz = jnp.pad(g["fz"], ((0, 0), (0, LANES - N_HEADS_FOX))).astype(BF16)
    bf = jnp.pad(b_forget[0].astype(F32), (0, LANES - N_HEADS_FOX)).reshape(1, LANES)
    invf = (ROPE_THETA ** (-jnp.arange(0, HEAD_DIM, 2, dtype=F32) / HEAD_DIM)).reshape(HEAD_DIM // 2, 1)

    fqT, fk, fvT, fgT, dqT, dk, dvT, dgT = _project(
        x2d, norm_g[0].reshape(1, D_MODEL), wt, wfk, wfz, bf, invf, _bias_selector())

    mixT_fox = _fox_attention(fqT, fk, fvT, fgT)
    row = lambda a: a[0].astype(F32).reshape(1, HEAD_DIM)
    mixT_diff = _diff_attention(dqT, dk, dvT, dgT, row(lambda_q1), row(lambda_k1), row(lambda_q2), row(lambda_k2),
                                subln_g[0].astype(F32).reshape(DIFF_V_DIM, 1))
    out = _out_project(mixT_fox, mixT_diff, x2d, w_out[0].astype(BF16), final_g.reshape(1, D_MODEL))
    return out[None]
```

```python
import math

import jax
import jax.numpy as jnp
import numpy as np
from jax import lax
from jax.experimental import pallas as pl
from jax.experimental.pallas import tpu as pltpu

D_MODEL = 1024
HEAD_DIM = 64
N_HEADS_FOX = 8
N_HEADS_DIFF = 4
N_MAPS_DIFF = 2 * N_HEADS_DIFF
DIFF_V_DIM = 2 * HEAD_DIM
WIDTH = 512
CHUNK = 64
ROPE_THETA = 10000.0
EPS = 1e-6
LAMBDA_INIT = 0.8 - 0.6 * math.exp(-0.3 * 0)
SCALE = HEAD_DIM ** -0.5
LOG2E = math.log2(math.e)
Q_SCALE = SCALE * LOG2E

LANES = 128
ROW_BLOCK = 512
FOX_BLOCK = 2048
DIFF_BLOCK = 1024
SUB_KEYS = 256
Q_TILE = 512
N_BIAS_PIECES = 3
BF16_SUBLANES = 16
FOX_V_ROWS = HEAD_DIM + BF16_SUBLANES
DIFF_V_ROWS = DIFF_V_DIM + BF16_SUBLANES
VMEM_LIMIT_BYTES = 56 * 1024 * 1024

F32 = jnp.float32
BF16 = jnp.bfloat16

_T_GROUPS = ("fq", "fv", "fg", "dq", "dk", "dv", "dg")
_T_OFF = {name: i * WIDTH for i, name in enumerate(_T_GROUPS)}


def _dot_nt(a, b):
    return lax.dot_general(a, b, (((1,), (1,)), ((), ())), preferred_element_type=F32)


def _dot(a, b):
    return jnp.dot(a, b, preferred_element_type=F32)


def _split_bf16(x):
    hi = x.astype(BF16)
    r = x - hi.astype(F32)
    mid = r.astype(BF16)
    lo = (r - mid.astype(F32)).astype(BF16)
    return hi, mid, lo


def _proj_kernel(x_ref, ng_ref, wt_ref, wfk_ref, wfz_ref, bf_ref, invf_ref, sel_ref,
                 fqT_ref, fk_ref, fvT_ref, fgT_ref, dqT_ref, dk_ref, dvT_ref, dgT_ref,
                 carry_ref):
    i = pl.program_id(0)
    tm = x_ref.shape[0]

    @pl.when(i == 0)
    def _():
        carry_ref[...] = jnp.zeros_like(carry_ref)

    x = x_ref[...]
    u = x * lax.rsqrt(jnp.mean(x * x, axis=-1, keepdims=True) + EPS) * ng_ref[...]
    ub = u.astype(BF16)

    def proj_t(name, width=WIDTH):
        off = _T_OFF[name]
        return _dot_nt(wt_ref[off:off + width, :], ub)

    qT = proj_t("fq") * Q_SCALE
    row = lax.broadcasted_iota(jnp.int32, (LANES - HEAD_DIM, tm), 0)
    ones_rows = jnp.where(row < N_BIAS_PIECES, 1.0, 0.0).astype(BF16)
    for h in range(N_HEADS_FOX):
        fqT_ref[0, h * LANES:h * LANES + HEAD_DIM, :] = qT[h * HEAD_DIM:(h + 1) * HEAD_DIM, :].astype(BF16)
        fqT_ref[0, h * LANES + HEAD_DIM:(h + 1) * LANES, :] = ones_rows
    row = lax.broadcasted_iota(jnp.int32, (BF16_SUBLANES, tm), 0)
    denom_rows = jnp.where(row == 0, 1.0, 0.0).astype(BF16)
    vT = proj_t("fv").astype(BF16)
    for h in range(N_HEADS_FOX):
        fvT_ref[0, h * FOX_V_ROWS:h * FOX_V_ROWS + HEAD_DIM, :] = vT[h * HEAD_DIM:(h + 1) * HEAD_DIM, :]
        fvT_ref[0, h * FOX_V_ROWS + HEAD_DIM:(h + 1) * FOX_V_ROWS, :] = denom_rows
    fg = proj_t("fg")
    fgT_ref[0] = (fg * jax.nn.sigmoid(fg)).astype(BF16)

    z = _dot(ub, wfz_ref[...]) + bf_ref[...]
    logf = jax.nn.log_sigmoid(z)
    r = lax.broadcasted_iota(jnp.int32, (tm, tm), 0)
    c = lax.broadcasted_iota(jnp.int32, (tm, tm), 1)
    tri = jnp.where(r >= c, 1.0, 0.0).astype(BF16)
    hi, mid, lo = _split_bf16(logf)
    cum = (_dot(tri, hi) + _dot(tri, mid)) + _dot(tri, lo) + carry_ref[...]
    carry_ref[...] = cum[tm - 1:tm, :]
    nh, nm, nl = _split_bf16(cum * (-LOG2E))
    bias = (_dot(nh, sel_ref[0]) + _dot(nm, sel_ref[1])) + _dot(nl, sel_ref[2])
    fk_ref[...] = (_dot(ub, wfk_ref[...]) + bias).astype(BF16)

    pos = (i * tm + lax.broadcasted_iota(jnp.int32, (1, tm), 1)).astype(F32)
    ang = pos * invf_ref[...]
    cos, sin = jnp.cos(ang), jnp.sin(ang)
    half = HEAD_DIM // 2

    def rope_t(xt):
        x1, x2 = xt[:half, :], xt[half:, :]
        return jnp.concatenate([x1 * cos - x2 * sin, x2 * cos + x1 * sin], axis=0)

    qT = proj_t("dq")
    zeros = jnp.zeros((HEAD_DIM, tm), BF16)
    for m in range(N_MAPS_DIFF):
        qm = (rope_t(qT[m * HEAD_DIM:(m + 1) * HEAD_DIM, :]) * Q_SCALE).astype(BF16)
        lo_rows, hi_rows = (qm, zeros) if m % 2 == 0 else (zeros, qm)
        dqT_ref[0, m * LANES:m * LANES + HEAD_DIM, :] = lo_rows
        dqT_ref[0, m * LANES + HEAD_DIM:(m + 1) * LANES, :] = hi_rows
    kT = proj_t("dk")
    kT = jnp.concatenate([rope_t(kT[m * HEAD_DIM:(m + 1) * HEAD_DIM, :]) for m in range(N_MAPS_DIFF)], axis=0)
    dk_ref[...] = kT.T.astype(BF16)
    vT = proj_t("dv").astype(BF16)
    for h in range(N_HEADS_DIFF):
        dvT_ref[0, h * DIFF_V_ROWS:h * DIFF_V_ROWS + DIFF_V_DIM, :] = vT[h * DIFF_V_DIM:(h + 1) * DIFF_V_DIM, :]
        dvT_ref[0, h * DIFF_V_ROWS + DIFF_V_DIM:(h + 1) * DIFF_V_ROWS, :] = denom_rows
    dg = proj_t("dg")
    dgT_ref[0] = (dg * jax.nn.sigmoid(dg)).astype(BF16)


def _project(x2d, norm_g, wt, wfk, wfz, bf, invf, sel):
    seq = x2d.shape[0]
    tm = ROW_BLOCK
    nblk = seq // tm
    const = lambda shape: pl.BlockSpec(shape, lambda i: (0,) * len(shape))
    chunk_t = lambda rows: pl.BlockSpec((1, rows, tm), lambda i: (i, 0, 0))
    chunked = lambda rows: jax.ShapeDtypeStruct((nblk, rows, tm), BF16)
    out_shape = (
        chunked(N_HEADS_FOX * LANES),
        jax.ShapeDtypeStruct((seq, N_HEADS_FOX * LANES), BF16),
        chunked(N_HEADS_FOX * FOX_V_ROWS),
        chunked(WIDTH),
        chunked(N_MAPS_DIFF * LANES),
        jax.ShapeDtypeStruct((seq, WIDTH), BF16),
        chunked(N_HEADS_DIFF * DIFF_V_ROWS),
        chunked(WIDTH),
    )
    return pl.pallas_call(
        _proj_kernel,
        grid=(nblk,),
        in_specs=[
            pl.BlockSpec((tm, D_MODEL), lambda i: (i, 0)),
            const((1, D_MODEL)),
            const(wt.shape), const(wfk.shape), const(wfz.shape), const(bf.shape),
            const(invf.shape), const(sel.shape),
        ],
        out_specs=(
            chunk_t(N_HEADS_FOX * LANES),
            pl.BlockSpec((tm, N_HEADS_FOX * LANES), lambda i: (i, 0)),
            chunk_t(N_HEADS_FOX * FOX_V_ROWS), chunk_t(WIDTH),
            chunk_t(N_MAPS_DIFF * LANES),
            pl.BlockSpec((tm, WIDTH), lambda i: (i, 0)),
            chunk_t(N_HEADS_DIFF * DIFF_V_ROWS), chunk_t(WIDTH),
        ),
        out_shape=out_shape,
        scratch_shapes=[pltpu.VMEM((1, LANES), F32)],
        compiler_params=pltpu.CompilerParams(
            dimension_semantics=("arbitrary",), vmem_limit_bytes=VMEM_LIMIT_BYTES),
        name="proj",
    )(x2d, norm_g, wt, wfk, wfz, bf, invf, sel)


def _attn_head(load_rhs, finalize, mask_fn, k_ref, vT_ref, rhs_ref, s_ref, mblk_ref, m_ref, acc_ref,
               *, blk, n_maps):
    nq = n_maps * blk
    v_rows, v_chunk = vT_ref.shape[1:]
    n_sub = blk // SUB_KEYS
    n_qblocks = k_ref.shape[0] // blk
    neg_inf = lambda cols: jnp.full((1, cols), -jnp.inf, F32)

    def tile_kind(diag, k0, q0):
        if not diag or k0 + SUB_KEYS <= q0:
            return "full"
        return "skip" if k0 >= q0 + Q_TILE else "mask"

    def step(consume, produce):
        if consume is not None:
            m_old = m_ref[...]
            m_new = jnp.maximum(m_old, mblk_ref[...])
            m_ref[...] = m_new
            alpha = jnp.exp2(m_old - m_new)
        for cq in range(nq // Q_TILE):
            cols = slice(cq * Q_TILE, (cq + 1) * Q_TILE)
            q0 = (cq * Q_TILE) % blk
            pv, m_next = None, None
            for j in range(n_sub):
                k0 = j * SUB_KEYS
                rows = slice(k0, k0 + SUB_KEYS)
                p_kind = tile_kind(produce[1], k0, q0) if produce is not None else "skip"
                c_kind = tile_kind(consume[1], k0, q0) if consume is not None else "skip"
                if p_kind != "skip":
                    row0 = produce[0] * blk + k0
                    if not isinstance(row0, int):
                        row0 = pl.multiple_of(row0, SUB_KEYS)
                    s = _dot(k_ref[pl.ds(row0, SUB_KEYS), :], rhs_ref[:, cols])
                    if p_kind == "mask":
                        s = jnp.where(mask_fn(k0, q0), s, -jnp.inf)
                if c_kind != "skip":
                    c, off = divmod(k0, v_chunk)
                    v_sub = vT_ref[consume[0] * (blk // v_chunk) + c, :, off:off + SUB_KEYS]
                    d = _dot(v_sub, jnp.exp2(s_ref[rows, cols] - m_new[:, cols]).astype(BF16))
                    pv = d if pv is None else pv + d
                if p_kind != "skip":
                    s_ref[rows, cols] = s
                    m_j = jnp.max(s, axis=0, keepdims=True)
                    m_next = m_j if m_next is None else jnp.maximum(m_next, m_j)
            if consume is not None:
                acc_ref[:, cols] = alpha[:, cols] * acc_ref[:, cols] + pv
            if produce is not None:
                mblk_ref[:, cols] = m_next

    def reset():
        m_ref[...] = neg_inf(nq)
        acc_ref[...] = jnp.zeros_like(acc_ref)

    def rest_of_query_block(q, produce_after):
        step((q, True), (0, False))
        lax.fori_loop(1, q, lambda kb, _: step((kb - 1, False), (kb, False)), None)
        if produce_after is not None:
            load_rhs(q + 1)
        step((q - 1, False), produce_after)
        finalize(q)

    reset()
    load_rhs(0)
    step(None, (0, True))
    load_rhs(1)
    step((0, True), (1, True))
    finalize(0)
    reset()

    def middle(q, _):
        rest_of_query_block(q, (q + 1, True))
        reset()

    lax.fori_loop(1, n_qblocks - 1, middle, None)
    rest_of_query_block(n_qblocks - 1, None)


def _fox_kernel(qT_ref, k_ref, vT_ref, gT_ref, o_ref, rhs_ref, s_ref, mblk_ref, m_ref, acc_ref):
    blk = s_ref.shape[0]
    chunk = qT_ref.shape[2]
    per_block = blk // chunk

    def load_rhs(q):
        for c in range(per_block):
            rhs_ref[:, c * chunk:(c + 1) * chunk] = qT_ref[q * per_block + c]

    def finalize(q):
        acc = acc_ref[...]
        o = acc[:HEAD_DIM, :] / acc[HEAD_DIM:HEAD_DIM + 1, :]
        for c in range(per_block):
            g = gT_ref[q * per_block + c].astype(F32)
            o_ref[q * per_block + c] = (o[:, c * chunk:(c + 1) * chunk] * g).astype(o_ref.dtype)

    def mask_fn(k0, q0):
        shape = (SUB_KEYS, Q_TILE)
        return k0 + lax.broadcasted_iota(jnp.int32, shape, 0) <= q0 + lax.broadcasted_iota(jnp.int32, shape, 1)

    _attn_head(load_rhs, finalize, mask_fn, k_ref, vT_ref, rhs_ref, s_ref, mblk_ref, m_ref, acc_ref,
               blk=blk, n_maps=1)


def _diff_kernel(q1T_ref, q2T_ref, k_ref, vT_ref, gT_ref, lq1_ref, lk1_ref, lq2_ref, lk2_ref, sg_ref, o_ref,
                 rhs_ref, s_ref, mblk_ref, m_ref, acc_ref):
    blk = s_ref.shape[0]
    chunk = q1T_ref.shape[2]
    per_block = blk // chunk

    def load_rhs(q):
        for mp, qT_ref in enumerate((q1T_ref, q2T_ref)):
            for c in range(per_block):
                rhs_ref[:, mp * blk + c * chunk:mp * blk + (c + 1) * chunk] = qT_ref[q * per_block + c]

    def finalize(q):
        acc = acc_ref[...]
        o = acc[:DIFF_V_DIM, :] / acc[DIFF_V_DIM:DIFF_V_DIM + 1, :]
        lam = (jnp.exp(jnp.sum(lq1_ref[...] * lk1_ref[...], axis=-1, keepdims=True))
               - jnp.exp(jnp.sum(lq2_ref[...] * lk2_ref[...], axis=-1, keepdims=True)) + LAMBDA_INIT)
        y = o[:, :blk] - lam * o[:, blk:]
        y = y * lax.rsqrt(jnp.mean(y * y, axis=0, keepdims=True) + EPS) * sg_ref[...]
        y = y * (1.0 - LAMBDA_INIT)
        for c in range(per_block):
            g = gT_ref[q * per_block + c].astype(F32)
            o_ref[q * per_block + c] = (y[:, c * chunk:(c + 1) * chunk] * g).astype(o_ref.dtype)

    def mask_fn(k0, q0):
        shape = (SUB_KEYS, Q_TILE)
        return ((k0 + lax.broadcasted_iota(jnp.int32, shape, 0)) // CHUNK
                <= (q0 + lax.broadcasted_iota(jnp.int32, shape, 1)) // CHUNK)

    _attn_head(load_rhs, finalize, mask_fn, k_ref, vT_ref, rhs_ref, s_ref, mblk_ref, m_ref, acc_ref,
               blk=blk, n_maps=2)


def _head_spec(arr, rows, row_block_of):
    return pl.BlockSpec((arr.shape[0], rows, arr.shape[2]), lambda h: (0, row_block_of(h), 0),
                        pipeline_mode=pl.Buffered(1))


def _attn_scratch(blk, n_maps, v_rows):
    nq = n_maps * blk
    return [pltpu.VMEM((LANES, nq), BF16), pltpu.VMEM((blk, nq), F32),
            pltpu.VMEM((1, nq), F32), pltpu.VMEM((1, nq), F32), pltpu.VMEM((v_rows, nq), F32)]


def _fox_attention(fqT, fk, fvT, fgT):
    seq = fk.shape[0]
    same = lambda h: h
    return pl.pallas_call(
        _fox_kernel,
        grid=(N_HEADS_FOX,),
        in_specs=[
            _head_spec(fqT, LANES, same),
            pl.BlockSpec((seq, LANES), lambda h: (0, h), pipeline_mode=pl.Buffered(1)),
            _head_spec(fvT, FOX_V_ROWS, same),
            _head_spec(fgT, HEAD_DIM, same),
        ],
        out_specs=pl.BlockSpec((fgT.shape[0], HEAD_DIM, fgT.shape[2]), lambda h: (0, h, 0)),
        out_shape=jax.ShapeDtypeStruct(fgT.shape, BF16),
        scratch_shapes=_attn_scratch(FOX_BLOCK, 1, FOX_V_ROWS),
        compiler_params=pltpu.CompilerParams(
            dimension_semantics=("arbitrary",), vmem_limit_bytes=VMEM_LIMIT_BYTES),
        name="fox_attn",
    )(fqT, fk, fvT, fgT)


def _diff_attention(dqT, dk, dvT, dgT, lq1, lk1, lq2, lk2, sg):
    seq = dk.shape[0]
    same = lambda h: h
    small = lambda a: pl.BlockSpec(a.shape, lambda h: (0, 0))
    return pl.pallas_call(
        _diff_kernel,
        grid=(N_HEADS_DIFF,),
        in_specs=[
            _head_spec(dqT, LANES, lambda h: 2 * h),
            _head_spec(dqT, LANES, lambda h: 2 * h + 1),
            pl.BlockSpec((seq, LANES), lambda h: (0, h), pipeline_mode=pl.Buffered(1)),
            _head_spec(dvT, DIFF_V_ROWS, same),
            _head_spec(dgT, DIFF_V_DIM, same),
            small(lq1), small(lk1), small(lq2), small(lk2), small(sg),
        ],
        out_specs=pl.BlockSpec((dgT.shape[0], DIFF_V_DIM, dgT.shape[2]), lambda h: (0, h, 0)),
        out_shape=jax.ShapeDtypeStruct(dgT.shape, BF16),
        scratch_shapes=_attn_scratch(DIFF_BLOCK, 2, DIFF_V_ROWS),
        compiler_params=pltpu.CompilerParams(
            dimension_semantics=("arbitrary",), vmem_limit_bytes=VMEM_LIMIT_BYTES),
        name="diff_attn",
    )(dqT, dqT, dk, dvT, dgT, lq1, lk1, lq2, lk2, sg)


def _out_kernel(mf_ref, md_ref, x_ref, w_ref, g_ref, o_ref):
    tn = (((0,), (0,)), ((), ()))
    y = lax.dot_general(mf_ref[0], w_ref[:WIDTH, :], tn, preferred_element_type=F32)
    y = y + lax.dot_general(md_ref[0], w_ref[WIDTH:, :], tn, preferred_element_type=F32)
    h = x_ref[...] + y
    o_ref[...] = h * lax.rsqrt(jnp.mean(h * h, axis=-1, keepdims=True) + EPS) * g_ref[...]


def _out_project(mixT_fox, mixT_diff, x2d, w_out, final_g):
    seq = x2d.shape[0]
    nblk, _, tm = mixT_fox.shape
    return pl.pallas_call(
        _out_kernel,
        grid=(nblk,),
        in_specs=[
            pl.BlockSpec((1, WIDTH, tm), lambda i: (i, 0, 0)),
            pl.BlockSpec((1, WIDTH, tm), lambda i: (i, 0, 0)),
            pl.BlockSpec((tm, D_MODEL), lambda i: (i, 0)),
            pl.BlockSpec(w_out.shape, lambda i: (0, 0)),
            pl.BlockSpec((1, D_MODEL), lambda i: (0, 0)),
        ],
        out_specs=pl.BlockSpec((tm, D_MODEL), lambda i: (i, 0)),
        out_shape=jax.ShapeDtypeStruct((seq, D_MODEL), F32),
        compiler_params=pltpu.CompilerParams(
            dimension_semantics=("arbitrary",), vmem_limit_bytes=VMEM_LIMIT_BYTES),
        name="out_proj",
    )(mixT_fox, mixT_diff, x2d, w_out, final_g)


def _bias_selector():
    sel = np.zeros((N_BIAS_PIECES, LANES, N_HEADS_FOX * LANES), np.float32)
    for p in range(N_BIAS_PIECES):
        for h in range(N_HEADS_FOX):
            sel[p, h, h * LANES + HEAD_DIM + p] = 1.0
    return jnp.asarray(sel, BF16)


def kernel(x, norm_g, w_in, b_forget, lambda_q1, lambda_k1, lambda_q2, lambda_k2, subln_g, w_out, final_g):
    batch, seq, _ = x.shape
    assert batch == 1
    for blk in (FOX_BLOCK, DIFF_BLOCK):
        assert seq % blk == 0 and seq // blk >= 3 and blk % ROW_BLOCK == 0
    x2d = x[0]
    w = w_in[0]
    sizes = (WIDTH, WIDTH, WIDTH, WIDTH, N_HEADS_FOX, WIDTH, WIDTH, WIDTH, WIDTH)
    offs = np.concatenate([[0], np.cumsum(sizes)])
    g = {n: w[:, offs[j]:offs[j + 1]] for j, n in enumerate(("fq", "fk", "fv", "fg", "fz", "dq", "dk", "dv", "dg"))}
    wt = jnp.concatenate([g[n] for n in _T_GROUPS], axis=1).T.astype(BF16)
    wfk = jnp.pad(g["fk"].reshape(D_MODEL, N_HEADS_FOX, HEAD_DIM),
                  ((0, 0), (0, 0), (0, LANES - HEAD_DIM))).reshape(D_MODEL, N_HEADS_FOX * LANES).astype(BF16)
    wfz = jnp.pad(g["fz"], ((0, 0), (0, LANES - N_HEADS_FOX))).astype(BF16)
    bf = jnp.pad(b_forget[0].astype(F32), (0, LANES - N_HEADS_FOX)).reshape(1, LANES)
    invf = (ROPE_THETA ** (-jnp.arange(0, HEAD_DIM, 2, dtype=F32) / HEAD_DIM)).reshape(HEAD_DIM // 2, 1)

    fqT, fk, fvT, fgT, dqT, dk, dvT, dgT = _project(
        x2d, norm_g[0].reshape(1, D_MODEL), wt, wfk, wfz, bf, invf, _bias_selector())

    mixT_fox = _fox_attention(fqT, fk, fvT, fgT)
    row = lambda a: a[0].astype(F32).reshape(1, HEAD_DIM)
    mixT_diff = _diff_attention(dqT, dk, dvT, dgT, row(lambda_q1), row(lambda_k1), row(lambda_q2), row(lambda_k2),
                                subln_g[0].astype(F32).reshape(DIFF_V_DIM, 1))
    out = _out_project(mixT_fox, mixT_diff, x2d, w_out[0].astype(BF16), final_g.reshape(1, D_MODEL))
    return out[None]
```

```python
import math

import jax
import jax.numpy as jnp
import numpy as np
from jax import lax
from jax.experimental import pallas as pl
from jax.experimental.pallas import tpu as pltpu

D_MODEL = 1024
HEAD_DIM = 64
N_HEADS_FOX = 8
N_HEADS_DIFF = 4
N_MAPS_DIFF = 2 * N_HEADS_DIFF
DIFF_V_DIM = 2 * HEAD_DIM
WIDTH = 512
CHUNK = 64
ROPE_THETA = 10000.0
EPS = 1e-6
LAMBDA_INIT = 0.8 - 0.6 * math.exp(-0.3 * 0)
SCALE = HEAD_DIM ** -0.5
LOG2E = math.log2(math.e)
Q_SCALE = SCALE * LOG2E

LANES = 128
ROW_BLOCK = 512
FOX_BLOCK = 2048
DIFF_BLOCK = 1024
SUB_KEYS = 256
Q_TILE = 512
N_BIAS_PIECES = 3
BF16_SUBLANES = 16
FOX_V_ROWS = HEAD_DIM + BF16_SUBLANES
DIFF_V_ROWS = DIFF_V_DIM + BF16_SUBLANES
VMEM_LIMIT_BYTES = 56 * 1024 * 1024

F32 = jnp.float32
BF16 = jnp.bfloat16

_T_GROUPS = ("fq", "fv", "fg", "dq", "dk", "dv", "dg")
_T_OFF = {name: i * WIDTH for i, name in enumerate(_T_GROUPS)}


def _dot_nt(a, b):
    return lax.dot_general(a, b, (((1,), (1,)), ((), ())), preferred_element_type=F32)


def _dot(a, b):
    return jnp.dot(a, b, preferred_element_type=F32)


def _split_bf16(x):
    hi = x.astype(BF16)
    r = x - hi.astype(F32)
    mid = r.astype(BF16)
    lo = (r - mid.astype(F32)).astype(BF16)
    return hi, mid, lo


def _proj_kernel(x_ref, ng_ref, wt_ref, wfk_ref, wfz_ref, bf_ref, invf_ref, sel_ref,
                 fqT_ref, fk_ref, fvT_ref, fgT_ref, dqT_ref, dk_ref, dvT_ref, dgT_ref,
                 carry_ref):
    i = pl.program_id(0)
    tm = x_ref.shape[0]

    @pl.when(i == 0)
    def _():
        carry_ref[...] = jnp.zeros_like(carry_ref)

    x = x_ref[...]
    u = x * lax.rsqrt(jnp.mean(x * x, axis=-1, keepdims=True) + EPS) * ng_ref[...]
    ub = u.astype(BF16)

    def proj_t(name, width=WIDTH):
        off = _T_OFF[name]
        return _dot_nt(wt_ref[off:off + width, :], ub)

    qT = proj_t("fq") * Q_SCALE
    row = lax.broadcasted_iota(jnp.int32, (LANES - HEAD_DIM, tm), 0)
    ones_rows = jnp.where(row < N_BIAS_PIECES, 1.0, 0.0).astype(BF16)
    for h in range(N_HEADS_FOX):
        fqT_ref[0, h * LANES:h * LANES + HEAD_DIM, :] = qT[h * HEAD_DIM:(h + 1) * HEAD_DIM, :].astype(BF16)
        fqT_ref[0, h * LANES + HEAD_DIM:(h + 1) * LANES, :] = ones_rows
    row = lax.broadcasted_iota(jnp.int32, (BF16_SUBLANES, tm), 0)
    denom_rows = jnp.where(row == 0, 1.0, 0.0).astype(BF16)
    vT = proj_t("fv").astype(BF16)
    for h in range(N_HEADS_FOX):
        fvT_ref[0, h * FOX_V_ROWS:h * FOX_V_ROWS + HEAD_DIM, :] = vT[h * HEAD_DIM:(h + 1) * HEAD_DIM, :]
        fvT_ref[0, h * FOX_V_ROWS + HEAD_DIM:(h + 1) * FOX_V_ROWS, :] = denom_rows
    fg = proj_t("fg")
    fgT_ref[0] = (fg * jax.nn.sigmoid(fg)).astype(BF16)

    z = _dot(ub, wfz_ref[...]) + bf_ref[...]
    logf = jax.nn.log_sigmoid(z)
    r = lax.broadcasted_iota(jnp.int32, (tm, tm), 0)
    c = lax.broadcasted_iota(jnp.int32, (tm, tm), 1)
    tri = jnp.where(r >= c, 1.0, 0.0).astype(BF16)
    hi, mid, lo = _split_bf16(logf)
    cum = (_dot(tri, hi) + _dot(tri, mid)) + _dot(tri, lo) + carry_ref[...]
    carry_ref[...] = cum[tm - 1:tm, :]
    nh, nm, nl = _split_bf16(cum * (-LOG2E))
    bias = (_dot(nh, sel_ref[0]) + _dot(nm, sel_ref[1])) + _dot(nl, sel_ref[2])
    fk_ref[...] = (_dot(ub, wfk_ref[...]) + bias).astype(BF16)

    pos = (i * tm + lax.broadcasted_iota(jnp.int32, (1, tm), 1)).astype(F32)
    ang = pos * invf_ref[...]
    cos, sin = jnp.cos(ang), jnp.sin(ang)
    half = HEAD_DIM // 2

    def rope_t(xt):
        x1, x2 = xt[:half, :], xt[half:, :]
        return jnp.concatenate([x1 * cos - x2 * sin, x2 * cos + x1 * sin], axis=0)

    qT = proj_t("dq")
    for m in range(N_MAPS_DIFF):
        rows = slice(m * HEAD_DIM, (m + 1) * HEAD_DIM)
        dqT_ref[0, rows, :] = (rope_t(qT[rows, :]) * Q_SCALE).astype(BF16)
    kT = proj_t("dk")
    kT = jnp.concatenate([rope_t(kT[m * HEAD_DIM:(m + 1) * HEAD_DIM, :]) for m in range(N_MAPS_DIFF)], axis=0)
    dk_ref[...] = kT.T.astype(BF16)
    vT = proj_t("dv").astype(BF16)
    for h in range(N_HEADS_DIFF):
        dvT_ref[0, h * DIFF_V_ROWS:h * DIFF_V_ROWS + DIFF_V_DIM, :] = vT[h * DIFF_V_DIM:(h + 1) * DIFF_V_DIM, :]
        dvT_ref[0, h * DIFF_V_ROWS + DIFF_V_DIM:(h + 1) * DIFF_V_ROWS, :] = denom_rows
    dg = proj_t("dg")
    dgT_ref[0] = (dg * jax.nn.sigmoid(dg)).astype(BF16)


def _project(x2d, norm_g, wt, wfk, wfz, bf, invf, sel):
    seq = x2d.shape[0]
    tm = ROW_BLOCK
    nblk = seq // tm
    const = lambda shape: pl.BlockSpec(shape, lambda i: (0,) * len(shape))
    chunk_t = lambda rows: pl.BlockSpec((1, rows, tm), lambda i: (i, 0, 0))
    chunked = lambda rows: jax.ShapeDtypeStruct((nblk, rows, tm), BF16)
    out_shape = (
        chunked(N_HEADS_FOX * LANES),
        jax.ShapeDtypeStruct((seq, N_HEADS_FOX * LANES), BF16),
        chunked(N_HEADS_FOX * FOX_V_ROWS),
        chunked(WIDTH),
        chunked(WIDTH),
        jax.ShapeDtypeStruct((seq, WIDTH), BF16),
        chunked(N_HEADS_DIFF * DIFF_V_ROWS),
        chunked(WIDTH),
    )
    return pl.pallas_call(
        _proj_kernel,
        grid=(nblk,),
        in_specs=[
            pl.BlockSpec((tm, D_MODEL), lambda i: (i, 0)),
            const((1, D_MODEL)),
            const(wt.shape), const(wfk.shape), const(wfz.shape), const(bf.shape),
            const(invf.shape), const(sel.shape),
        ],
        out_specs=(
            chunk_t(N_HEADS_FOX * LANES),
            pl.BlockSpec((tm, N_HEADS_FOX * LANES), lambda i: (i, 0)),
            chunk_t(N_HEADS_FOX * FOX_V_ROWS), chunk_t(WIDTH),
            chunk_t(WIDTH),
            pl.BlockSpec((tm, WIDTH), lambda i: (i, 0)),
            chunk_t(N_HEADS_DIFF * DIFF_V_ROWS), chunk_t(WIDTH),
        ),
        out_shape=out_shape,
        scratch_shapes=[pltpu.VMEM((1, LANES), F32)],
        compiler_params=pltpu.CompilerParams(
            dimension_semantics=("arbitrary",), vmem_limit_bytes=VMEM_LIMIT_BYTES),
        name="proj",
    )(x2d, norm_g, wt, wfk, wfz, bf, invf, sel)


def _attn_head(load_rhs, finalize, mask_fn, k_ref, vT_ref, rhs_ref, s_ref, mblk_ref, m_ref, acc_ref,
               *, blk, n_maps):
    nq = n_maps * blk
    v_rows, v_chunk = vT_ref.shape[1:]
    n_sub = blk // SUB_KEYS
    n_qblocks = k_ref.shape[0] // blk
    neg_inf = lambda cols: jnp.full((1, cols), -jnp.inf, F32)

    def tile_kind(diag, k0, q0):
        if not diag or k0 + SUB_KEYS <= q0:
            return "full"
        return "skip" if k0 >= q0 + Q_TILE else "mask"

    def step(consume, produce):
        if consume is not None:
            m_old = m_ref[...]
            m_new = jnp.maximum(m_old, mblk_ref[...])
            m_ref[...] = m_new
            alpha = jnp.exp2(m_old - m_new)
        for cq in range(nq // Q_TILE):
            cols = slice(cq * Q_TILE, (cq + 1) * Q_TILE)
            q0 = (cq * Q_TILE) % blk
            pv, m_next = None, None
            for j in range(n_sub):
                k0 = j * SUB_KEYS
                rows = slice(k0, k0 + SUB_KEYS)
                p_kind = tile_kind(produce[1], k0, q0) if produce is not None else "skip"
                c_kind = tile_kind(consume[1], k0, q0) if consume is not None else "skip"
                if p_kind != "skip":
                    row0 = produce[0] * blk + k0
                    if not isinstance(row0, int):
                        row0 = pl.multiple_of(row0, SUB_KEYS)
                    s = _dot(k_ref[pl.ds(row0, SUB_KEYS), :], rhs_ref[:, cols])
                    if p_kind == "mask":
                        s = jnp.where(mask_fn(k0, q0), s, -jnp.inf)
                if c_kind != "skip":
                    c, off = divmod(k0, v_chunk)
                    v_sub = vT_ref[consume[0] * (blk // v_chunk) + c, :, off:off + SUB_KEYS]
                    d = _dot(v_sub, jnp.exp2(s_ref[rows, cols] - m_new[:, cols]).astype(BF16))
                    pv = d if pv is None else pv + d
                if p_kind != "skip":
                    s_ref[rows, cols] = s
                    m_j = jnp.max(s, axis=0, keepdims=True)
                    m_next = m_j if m_next is None else jnp.maximum(m_next, m_j)
            if consume is not None:
                acc_ref[:, cols] = alpha[:, cols] * acc_ref[:, cols] + pv
            if produce is not None:
                mblk_ref[:, cols] = m_next

    def reset():
        m_ref[...] = neg_inf(nq)
        acc_ref[...] = jnp.zeros_like(acc_ref)

    def rest_of_query_block(q, produce_after):
        step((q, True), (0, False))
        lax.fori_loop(1, q, lambda kb, _: step((kb - 1, False), (kb, False)), None)
        if produce_after is not None:
            load_rhs(q + 1)
        step((q - 1, False), produce_after)
        finalize(q)

    reset()
    load_rhs(0)
    step(None, (0, True))
    load_rhs(1)
    step((0, True), (1, True))
    finalize(0)
    reset()

    def middle(q, _):
        rest_of_query_block(q, (q + 1, True))
        reset()

    lax.fori_loop(1, n_qblocks - 1, middle, None)
    rest_of_query_block(n_qblocks - 1, None)


def _fox_kernel(qT_ref, k_ref, vT_ref, o_ref, rhs_ref, s_ref, mblk_ref, m_ref, acc_ref):
    blk = s_ref.shape[0]
    chunk = qT_ref.shape[2]
    per_block = blk // chunk

    def load_rhs(q):
        for c in range(per_block):
            rhs_ref[:, c * chunk:(c + 1) * chunk] = qT_ref[q * per_block + c]

    def finalize(q):
        acc = acc_ref[...]
        o = acc[:HEAD_DIM, :] / acc[HEAD_DIM:HEAD_DIM + 1, :]
        for c in range(per_block):
            o_ref[q * per_block + c] = o[:, c * chunk:(c + 1) * chunk].astype(o_ref.dtype)

    def mask_fn(k0, q0):
        shape = (SUB_KEYS, Q_TILE)
        return k0 + lax.broadcasted_iota(jnp.int32, shape, 0) <= q0 + lax.broadcasted_iota(jnp.int32, shape, 1)

    _attn_head(load_rhs, finalize, mask_fn, k_ref, vT_ref, rhs_ref, s_ref, mblk_ref, m_ref, acc_ref,
               blk=blk, n_maps=1)


def _diff_kernel(qT_ref, k_ref, vT_ref, lq1_ref, lk1_ref, lq2_ref, lk2_ref, sg_ref, o_ref,
                 rhs_ref, s_ref, mblk_ref, m_ref, acc_ref):
    blk = s_ref.shape[0]
    chunk = qT_ref.shape[2]
    per_block = blk // chunk

    def load_rhs(q):
        zeros = jnp.zeros((HEAD_DIM, chunk), BF16)
        for c in range(per_block):
            q12 = qT_ref[q * per_block + c]
            rhs_ref[:, c * chunk:(c + 1) * chunk] = jnp.concatenate([q12[:HEAD_DIM], zeros], axis=0)
            rhs_ref[:, blk + c * chunk:blk + (c + 1) * chunk] = jnp.concatenate([zeros, q12[HEAD_DIM:]], axis=0)

    def finalize(q):
        acc = acc_ref[...]
        o = acc[:DIFF_V_DIM, :] / acc[DIFF_V_DIM:DIFF_V_DIM + 1, :]
        lam = (jnp.exp(jnp.sum(lq1_ref[...] * lk1_ref[...], axis=-1, keepdims=True))
               - jnp.exp(jnp.sum(lq2_ref[...] * lk2_ref[...], axis=-1, keepdims=True)) + LAMBDA_INIT)
        y = o[:, :blk] - lam * o[:, blk:]
        y = y * lax.rsqrt(jnp.mean(y * y, axis=0, keepdims=True) + EPS) * sg_ref[...]
        y = y * (1.0 - LAMBDA_INIT)
        for c in range(per_block):
            o_ref[q * per_block + c] = y[:, c * chunk:(c + 1) * chunk].astype(o_ref.dtype)

    def mask_fn(k0, q0):
        shape = (SUB_KEYS, Q_TILE)
        return ((k0 + lax.broadcasted_iota(jnp.int32, shape, 0)) // CHUNK
                <= (q0 + lax.broadcasted_iota(jnp.int32, shape, 1)) // CHUNK)

    _attn_head(load_rhs, finalize, mask_fn, k_ref, vT_ref, rhs_ref, s_ref, mblk_ref, m_ref, acc_ref,
               blk=blk, n_maps=2)


def _head_spec(arr, rows):
    return pl.BlockSpec((arr.shape[0], rows, arr.shape[2]), lambda h: (0, h, 0))


def _attn_scratch(blk, n_maps, v_rows):
    nq = n_maps * blk
    return [pltpu.VMEM((LANES, nq), BF16), pltpu.VMEM((blk, nq), F32),
            pltpu.VMEM((1, nq), F32), pltpu.VMEM((1, nq), F32), pltpu.VMEM((v_rows, nq), F32)]


def _fox_attention(fqT, fk, fvT):
    seq = fk.shape[0]
    out_shape = jax.ShapeDtypeStruct((fqT.shape[0], WIDTH, fqT.shape[2]), BF16)
    return pl.pallas_call(
        _fox_kernel,
        grid=(N_HEADS_FOX,),
        in_specs=[
            _head_spec(fqT, LANES),
            pl.BlockSpec((seq, LANES), lambda h: (0, h)),
            _head_spec(fvT, FOX_V_ROWS),
        ],
        out_specs=_head_spec(out_shape, HEAD_DIM),
        out_shape=out_shape,
        scratch_shapes=_attn_scratch(FOX_BLOCK, 1, FOX_V_ROWS),
        compiler_params=pltpu.CompilerParams(
            dimension_semantics=("arbitrary",), vmem_limit_bytes=VMEM_LIMIT_BYTES),
        name="fox_attn",
    )(fqT, fk, fvT)


def _diff_attention(dqT, dk, dvT, lq1, lk1, lq2, lk2, sg):
    seq = dk.shape[0]
    small = lambda a: pl.BlockSpec(a.shape, lambda h: (0, 0))
    out_shape = jax.ShapeDtypeStruct(dqT.shape, BF16)
    return pl.pallas_call(
        _diff_kernel,
        grid=(N_HEADS_DIFF,),
        in_specs=[
            _head_spec(dqT, 2 * HEAD_DIM),
            pl.BlockSpec((seq, LANES), lambda h: (0, h)),
            _head_spec(dvT, DIFF_V_ROWS),
            small(lq1), small(lk1), small(lq2), small(lk2), small(sg),
        ],
        out_specs=_head_spec(out_shape, DIFF_V_DIM),
        out_shape=out_shape,
        scratch_shapes=_attn_scratch(DIFF_BLOCK, 2, DIFF_V_ROWS),
        compiler_params=pltpu.CompilerParams(
            dimension_semantics=("arbitrary",), vmem_limit_bytes=VMEM_LIMIT_BYTES),
        name="diff_attn",
    )(dqT, dk, dvT, lq1, lk1, lq2, lk2, sg)


def _out_kernel(yf_ref, yd_ref, gf_ref, gd_ref, x_ref, w_ref, g_ref, o_ref):
    tn = (((0,), (0,)), ((), ()))
    gated = lambda y_ref, gate_ref: (y_ref[0].astype(F32) * gate_ref[0].astype(F32)).astype(BF16)
    y = lax.dot_general(gated(yf_ref, gf_ref), w_ref[:WIDTH, :], tn, preferred_element_type=F32)
    y = y + lax.dot_general(gated(yd_ref, gd_ref), w_ref[WIDTH:, :], tn, preferred_element_type=F32)
    h = x_ref[...] + y
    o_ref[...] = h * lax.rsqrt(jnp.mean(h * h, axis=-1, keepdims=True) + EPS) * g_ref[...]


def _out_project(yT_fox, yT_diff, fgT, dgT, x2d, w_out, final_g):
    seq = x2d.shape[0]
    nblk, _, tm = yT_fox.shape
    chunk = pl.BlockSpec((1, WIDTH, tm), lambda i: (i, 0, 0))
    return pl.pallas_call(
        _out_kernel,
        grid=(nblk,),
        in_specs=[
            chunk, chunk, chunk, chunk,
            pl.BlockSpec((tm, D_MODEL), lambda i: (i, 0)),
            pl.BlockSpec(w_out.shape, lambda i: (0, 0)),
            pl.BlockSpec((1, D_MODEL), lambda i: (0, 0)),
        ],
        out_specs=pl.BlockSpec((tm, D_MODEL), lambda i: (i, 0)),
        out_shape=jax.ShapeDtypeStruct((seq, D_MODEL), F32),
        compiler_params=pltpu.CompilerParams(
            dimension_semantics=("arbitrary",), vmem_limit_bytes=VMEM_LIMIT_BYTES),
        name="out_proj",
    )(yT_fox, yT_diff, fgT, dgT, x2d, w_out, final_g)


def _bias_selector():
    sel = np.zeros((N_BIAS_PIECES, LANES, N_HEADS_FOX * LANES), np.float32)
    for p in range(N_BIAS_PIECES):
        for h in range(N_HEADS_FOX):
            sel[p, h, h * LANES + HEAD_DIM + p] = 1.0
    return jnp.asarray(sel, BF16)


def kernel(x, norm_g, w_in, b_forget, lambda_q1, lambda_k1, lambda_q2, lambda_k2, subln_g, w_out, final_g):
    batch, seq, _ = x.shape
    assert batch == 1
    for blk in (FOX_BLOCK, DIFF_BLOCK):
        assert seq % blk == 0 and seq // blk >= 3 and blk % ROW_BLOCK == 0
    x2d = x[0]
    w = w_in[0]
    sizes = (WIDTH, WIDTH, WIDTH, WIDTH, N_HEADS_FOX, WIDTH, WIDTH, WIDTH, WIDTH)
    offs = np.concatenate([[0], np.cumsum(sizes)])
    g = {n: w[:, offs[j]:offs[j + 1]] for j, n in enumerate(("fq", "fk", "fv", "fg", "fz", "dq", "dk", "dv", "dg"))}
    wt = jnp.concatenate([g[n] for n in _T_GROUPS], axis=1).T.astype(BF16)
    wfk = jnp.pad(g["fk"].reshape(D_MODEL, N_HEADS_FOX, HEAD_DIM),
                  ((0, 0), (0, 0), (0, LANES - HEAD_DIM))).reshape(D_MODEL, N_HEADS_FOX * LANES).astype(BF16)
    wfz = jnp.pad(g["fz"], ((0, 0), (0, LANES - N_HEADS_FOX))).astype(BF16)
    bf = jnp.pad(b_forget[0].astype(F32), (0, LANES - N_HEADS_FOX)).reshape(1, LANES)
    invf = (ROPE_THETA ** (-jnp.arange(0, HEAD_DIM, 2, dtype=F32) / HEAD_DIM)).reshape(HEAD_DIM // 2, 1)

    fqT, fk, fvT, fgT, dqT, dk, dvT, dgT = _project(
        x2d, norm_g[0].reshape(1, D_MODEL), wt, wfk, wfz, bf, invf, _bias_selector())

    yT_fox = _fox_attention(fqT, fk, fvT)
    row = lambda a: a[0].astype(F32).reshape(1, HEAD_DIM)
    yT_diff = _diff_attention(dqT, dk, dvT, row(lambda_q1), row(lambda_k1), row(lambda_q2), row(lambda_k2),
                              subln_g[0].astype(F32).reshape(DIFF_V_DIM, 1))
    out = _out_project(yT_fox, yT_diff, fgT, dgT, x2d, w_out[0].astype(BF16), final_g.reshape(1, D_MODEL))
    return out[None]
```

```python
import math

import jax
import jax.numpy as jnp
import numpy as np
from jax import lax
from jax.experimental import pallas as pl
from jax.experimental.pallas import tpu as pltpu

D_MODEL = 1024
HEAD_DIM = 64
N_HEADS_FOX = 8
N_HEADS_DIFF = 4
N_MAPS_DIFF = 2 * N_HEADS_DIFF
DIFF_V_DIM = 2 * HEAD_DIM
WIDTH = 512
CHUNK = 64
ROPE_THETA = 10000.0
EPS = 1e-6
LAMBDA_INIT = 0.8 - 0.6 * math.exp(-0.3 * 0)
SCALE = HEAD_DIM ** -0.5
LOG2E = math.log2(math.e)
Q_SCALE = SCALE * LOG2E

LANES = 128
ROW_BLOCK = 512
FOX_BLOCK = 2048
DIFF_BLOCK = 1024
SUB_KEYS = 256
FOX_Q_TILE = 512
DIFF_Q_TILE = 256
N_BIAS_PIECES = 3
BF16_SUBLANES = 16
FOX_V_ROWS = HEAD_DIM + BF16_SUBLANES
DIFF_V_ROWS = DIFF_V_DIM + BF16_SUBLANES
VMEM_LIMIT_BYTES = 56 * 1024 * 1024

F32 = jnp.float32
BF16 = jnp.bfloat16

_T_GROUPS = ("fq", "fv", "fg", "dq", "dk", "dv", "dg")
_T_OFF = {name: i * WIDTH for i, name in enumerate(_T_GROUPS)}


def _dot_nt(a, b):
    return lax.dot_general(a, b, (((1,), (1,)), ((), ())), preferred_element_type=F32)


def _dot(a, b):
    return jnp.dot(a, b, preferred_element_type=F32)


def _split_bf16(x):
    hi = x.astype(BF16)
    r = x - hi.astype(F32)
    mid = r.astype(BF16)
    lo = (r - mid.astype(F32)).astype(BF16)
    return hi, mid, lo


def _proj_kernel(x_ref, ng_ref, wt_ref, wfk_ref, wfz_ref, bf_ref, invf_ref, sel_ref,
                 fqT_ref, fk_ref, fvT_ref, fgT_ref, dqT_ref, dk_ref, dvT_ref, dgT_ref,
                 carry_ref):
    i = pl.program_id(0)
    tm = x_ref.shape[0]

    @pl.when(i == 0)
    def _():
        carry_ref[...] = jnp.zeros_like(carry_ref)

    x = x_ref[...]
    u = x * lax.rsqrt(jnp.mean(x * x, axis=-1, keepdims=True) + EPS) * ng_ref[...]
    ub = u.astype(BF16)

    def proj_t(name, width=WIDTH):
        off = _T_OFF[name]
        return _dot_nt(wt_ref[off:off + width, :], ub)

    qT = proj_t("fq") * Q_SCALE
    row = lax.broadcasted_iota(jnp.int32, (LANES - HEAD_DIM, tm), 0)
    ones_rows = jnp.where(row < N_BIAS_PIECES, 1.0, 0.0).astype(BF16)
    for h in range(N_HEADS_FOX):
        fqT_ref[0, h * LANES:h * LANES + HEAD_DIM, :] = qT[h * HEAD_DIM:(h + 1) * HEAD_DIM, :].astype(BF16)
        fqT_ref[0, h * LANES + HEAD_DIM:(h + 1) * LANES, :] = ones_rows
    row = lax.broadcasted_iota(jnp.int32, (BF16_SUBLANES, tm), 0)
    denom_rows = jnp.where(row == 0, 1.0, 0.0).astype(BF16)
    vT = proj_t("fv").astype(BF16)
    for h in range(N_HEADS_FOX):
        fvT_ref[0, h * FOX_V_ROWS:h * FOX_V_ROWS + HEAD_DIM, :] = vT[h * HEAD_DIM:(h + 1) * HEAD_DIM, :]
        fvT_ref[0, h * FOX_V_ROWS + HEAD_DIM:(h + 1) * FOX_V_ROWS, :] = denom_rows
    fg = proj_t("fg")
    fgT_ref[0] = (fg * jax.nn.sigmoid(fg)).astype(BF16)

    z = _dot(ub, wfz_ref[...]) + bf_ref[...]
    logf = jax.nn.log_sigmoid(z)
    r = lax.broadcasted_iota(jnp.int32, (tm, tm), 0)
    c = lax.broadcasted_iota(jnp.int32, (tm, tm), 1)
    tri = jnp.where(r >= c, 1.0, 0.0).astype(BF16)
    hi, mid, lo = _split_bf16(logf)
    cum = (_dot(tri, hi) + _dot(tri, mid)) + _dot(tri, lo) + carry_ref[...]
    carry_ref[...] = cum[tm - 1:tm, :]
    nh, nm, nl = _split_bf16(cum * (-LOG2E))
    bias = (_dot(nh, sel_ref[0]) + _dot(nm, sel_ref[1])) + _dot(nl, sel_ref[2])
    fk_ref[...] = (_dot(ub, wfk_ref[...]) + bias).astype(BF16)

    pos = (i * tm + lax.broadcasted_iota(jnp.int32, (1, tm), 1)).astype(F32)
    ang = pos * invf_ref[...]
    cos, sin = jnp.cos(ang), jnp.sin(ang)
    half = HEAD_DIM // 2

    def rope_t(xt):
        x1, x2 = xt[:half, :], xt[half:, :]
        return jnp.concatenate([x1 * cos - x2 * sin, x2 * cos + x1 * sin], axis=0)

    qT = proj_t("dq")
    for m in range(N_MAPS_DIFF):
        rows = slice(m * HEAD_DIM, (m + 1) * HEAD_DIM)
        dqT_ref[0, rows, :] = (rope_t(qT[rows, :]) * Q_SCALE).astype(BF16)
    kT = proj_t("dk")
    kT = jnp.concatenate([rope_t(kT[m * HEAD_DIM:(m + 1) * HEAD_DIM, :]) for m in range(N_MAPS_DIFF)], axis=0)
    dk_ref[...] = kT.T.astype(BF16)
    vT = proj_t("dv").astype(BF16)
    for h in range(N_HEADS_DIFF):
        dvT_ref[0, h * DIFF_V_ROWS:h * DIFF_V_ROWS + DIFF_V_DIM, :] = vT[h * DIFF_V_DIM:(h + 1) * DIFF_V_DIM, :]
        dvT_ref[0, h * DIFF_V_ROWS + DIFF_V_DIM:(h + 1) * DIFF_V_ROWS, :] = denom_rows
    dg = proj_t("dg")
    dgT_ref[0] = (dg * jax.nn.sigmoid(dg)).astype(BF16)


def _project(x2d, norm_g, wt, wfk, wfz, bf, invf, sel):
    seq = x2d.shape[0]
    tm = ROW_BLOCK
    nblk = seq // tm
    const = lambda shape: pl.BlockSpec(shape, lambda i: (0,) * len(shape))
    chunk_t = lambda rows: pl.BlockSpec((1, rows, tm), lambda i: (i, 0, 0))
    chunked = lambda rows: jax.ShapeDtypeStruct((nblk, rows, tm), BF16)
    out_shape = (
        chunked(N_HEADS_FOX * LANES),
        jax.ShapeDtypeStruct((seq, N_HEADS_FOX * LANES), BF16),
        chunked(N_HEADS_FOX * FOX_V_ROWS),
        chunked(WIDTH),
        chunked(WIDTH),
        jax.ShapeDtypeStruct((seq, WIDTH), BF16),
        chunked(N_HEADS_DIFF * DIFF_V_ROWS),
        chunked(WIDTH),
    )
    return pl.pallas_call(
        _proj_kernel,
        grid=(nblk,),
        in_specs=[
            pl.BlockSpec((tm, D_MODEL), lambda i: (i, 0)),
            const((1, D_MODEL)),
            const(wt.shape), const(wfk.shape), const(wfz.shape), const(bf.shape),
            const(invf.shape), const(sel.shape),
        ],
        out_specs=(
            chunk_t(N_HEADS_FOX * LANES),
            pl.BlockSpec((tm, N_HEADS_FOX * LANES), lambda i: (i, 0)),
            chunk_t(N_HEADS_FOX * FOX_V_ROWS), chunk_t(WIDTH),
            chunk_t(WIDTH),
            pl.BlockSpec((tm, WIDTH), lambda i: (i, 0)),
            chunk_t(N_HEADS_DIFF * DIFF_V_ROWS), chunk_t(WIDTH),
        ),
        out_shape=out_shape,
        scratch_shapes=[pltpu.VMEM((1, LANES), F32)],
        compiler_params=pltpu.CompilerParams(
            dimension_semantics=("arbitrary",), vmem_limit_bytes=VMEM_LIMIT_BYTES),
        name="proj",
    )(x2d, norm_g, wt, wfk, wfz, bf, invf, sel)


def _attn_head(load_rhs, finalize, mask_fn, k_ref, vT_ref, rhs_ref, s_ref, mblk_ref, m_ref, acc_ref,
               *, blk, n_maps, q_tile):
    nq = n_maps * blk
    v_rows, v_chunk = vT_ref.shape[1:]
    n_sub = blk // SUB_KEYS
    n_qblocks = k_ref.shape[0] // blk
    neg_inf = lambda cols: jnp.full((1, cols), -jnp.inf, F32)

    def tile_kind(diag, k0, q0):
        if not diag or k0 + SUB_KEYS <= q0:
            return "full"
        return "skip" if k0 >= q0 + q_tile else "mask"

    def step(consume, produce):
        if consume is not None:
            m_old = m_ref[...]
            m_new = jnp.maximum(m_old, mblk_ref[...])
            m_ref[...] = m_new
            alpha = jnp.exp2(m_old - m_new)
        for cq in range(nq // q_tile):
            cols = slice(cq * q_tile, (cq + 1) * q_tile)
            q0 = (cq * q_tile) % blk
            pv, m_next = None, None
            for j in range(n_sub):
                k0 = j * SUB_KEYS
                rows = slice(k0, k0 + SUB_KEYS)
                p_kind = tile_kind(produce[1], k0, q0) if produce is not None else "skip"
                c_kind = tile_kind(consume[1], k0, q0) if consume is not None else "skip"
                if p_kind != "skip":
                    row0 = produce[0] * blk + k0
                    if not isinstance(row0, int):
                        row0 = pl.multiple_of(row0, SUB_KEYS)
                    s = _dot(k_ref[pl.ds(row0, SUB_KEYS), :], rhs_ref[:, cols])
                    if p_kind == "mask":
                        key_pos = k0 + lax.broadcasted_iota(jnp.int32, s.shape, 0)
                        query_pos = q0 + lax.broadcasted_iota(jnp.int32, s.shape, 1)
                        s = jnp.where(mask_fn(key_pos, query_pos), s, -jnp.inf)
                if c_kind != "skip":
                    c, off = divmod(k0, v_chunk)
                    v_sub = vT_ref[consume[0] * (blk // v_chunk) + c, :, off:off + SUB_KEYS]
                    d = _dot(v_sub, jnp.exp2(s_ref[rows, cols] - m_new[:, cols]).astype(BF16))
                    pv = d if pv is None else pv + d
                if p_kind != "skip":
                    s_ref[rows, cols] = s
                    m_j = jnp.max(s, axis=0, keepdims=True)
                    m_next = m_j if m_next is None else jnp.maximum(m_next, m_j)
            if consume is not None:
                acc_ref[:, cols] = alpha[:, cols] * acc_ref[:, cols] + pv
            if produce is not None:
                mblk_ref[:, cols] = m_next

    def reset():
        m_ref[...] = neg_inf(nq)
        acc_ref[...] = jnp.zeros_like(acc_ref)

    def rest_of_query_block(q, produce_after):
        step((q, True), (0, False))

        def two_steps(i, _):
            kb = 1 + 2 * i
            step((kb - 1, False), (kb, False))
            step((kb, False), (kb + 1, False))

        n_pairs = (q - 1) // 2
        lax.fori_loop(0, n_pairs, two_steps, None)
        lax.fori_loop(1 + 2 * n_pairs, q, lambda kb, _: step((kb - 1, False), (kb, False)), None)
        if produce_after is not None:
            load_rhs(q + 1)
        step((q - 1, False), produce_after)
        finalize(q)

    reset()
    load_rhs(0)
    step(None, (0, True))
    load_rhs(1)
    step((0, True), (1, True))
    finalize(0)
    reset()

    def middle(q, _):
        rest_of_query_block(q, (q + 1, True))
        reset()

    lax.fori_loop(1, n_qblocks - 1, middle, None)
    rest_of_query_block(n_qblocks - 1, None)


def _fox_kernel(qT_ref, k_ref, vT_ref, o_ref, rhs_ref, s_ref, mblk_ref, m_ref, acc_ref):
    blk = s_ref.shape[0]
    chunk = qT_ref.shape[2]
    per_block = blk // chunk

    def load_rhs(q):
        for c in range(per_block):
            rhs_ref[:, c * chunk:(c + 1) * chunk] = qT_ref[q * per_block + c]

    def finalize(q):
        acc = acc_ref[...]
        o = acc[:HEAD_DIM, :] / acc[HEAD_DIM:HEAD_DIM + 1, :]
        for c in range(per_block):
            o_ref[q * per_block + c] = o[:, c * chunk:(c + 1) * chunk].astype(o_ref.dtype)

    def mask_fn(key_pos, query_pos):
        return key_pos <= query_pos

    _attn_head(load_rhs, finalize, mask_fn, k_ref, vT_ref, rhs_ref, s_ref, mblk_ref, m_ref, acc_ref,
               blk=blk, n_maps=1, q_tile=FOX_Q_TILE)


def _diff_kernel(qT_ref, k_ref, vT_ref, lq1_ref, lk1_ref, lq2_ref, lk2_ref, sg_ref, o_ref,
                 rhs_ref, s_ref, mblk_ref, m_ref, acc_ref):
    blk = s_ref.shape[0]
    chunk = qT_ref.shape[2]
    per_block = blk // chunk

    def load_rhs(q):
        zeros = jnp.zeros((HEAD_DIM, chunk), BF16)
        for c in range(per_block):
            q12 = qT_ref[q * per_block + c]
            rhs_ref[:, c * chunk:(c + 1) * chunk] = jnp.concatenate([q12[:HEAD_DIM], zeros], axis=0)
            rhs_ref[:, blk + c * chunk:blk + (c + 1) * chunk] = jnp.concatenate([zeros, q12[HEAD_DIM:]], axis=0)

    def finalize(q):
        acc = acc_ref[...]
        o = acc[:DIFF_V_DIM, :] / acc[DIFF_V_DIM:DIFF_V_DIM + 1, :]
        lam = (jnp.exp(jnp.sum(lq1_ref[...] * lk1_ref[...], axis=-1, keepdims=True))
               - jnp.exp(jnp.sum(lq2_ref[...] * lk2_ref[...], axis=-1, keepdims=True)) + LAMBDA_INIT)
        y = o[:, :blk] - lam * o[:, blk:]
        y = y * lax.rsqrt(jnp.mean(y * y, axis=0, keepdims=True) + EPS) * sg_ref[...]
        y = y * (1.0 - LAMBDA_INIT)
        for c in range(per_block):
            o_ref[q * per_block + c] = y[:, c * chunk:(c + 1) * chunk].astype(o_ref.dtype)

    def mask_fn(key_pos, query_pos):
        return key_pos // CHUNK <= query_pos // CHUNK

    _attn_head(load_rhs, finalize, mask_fn, k_ref, vT_ref, rhs_ref, s_ref, mblk_ref, m_ref, acc_ref,
               blk=blk, n_maps=2, q_tile=DIFF_Q_TILE)


def _head_spec(arr, rows):
    return pl.BlockSpec((arr.shape[0], rows, arr.shape[2]), lambda h: (0, h, 0))


def _attn_scratch(blk, n_maps, v_rows):
    nq = n_maps * blk
    return [pltpu.VMEM((LANES, nq), BF16), pltpu.VMEM((blk, nq), F32),
            pltpu.VMEM((1, nq), F32), pltpu.VMEM((1, nq), F32), pltpu.VMEM((v_rows, nq), F32)]


def _fox_attention(fqT, fk, fvT):
    seq = fk.shape[0]
    out_shape = jax.ShapeDtypeStruct((fqT.shape[0], WIDTH, fqT.shape[2]), BF16)
    return pl.pallas_call(
        _fox_kernel,
        grid=(N_HEADS_FOX,),
        in_specs=[
            _head_spec(fqT, LANES),
            pl.BlockSpec((seq, LANES), lambda h: (0, h)),
            _head_spec(fvT, FOX_V_ROWS),
        ],
        out_specs=_head_spec(out_shape, HEAD_DIM),
        out_shape=out_shape,
        scratch_shapes=_attn_scratch(FOX_BLOCK, 1, FOX_V_ROWS),
        compiler_params=pltpu.CompilerParams(
            dimension_semantics=("arbitrary",), vmem_limit_bytes=VMEM_LIMIT_BYTES),
        name="fox_attn",
    )(fqT, fk, fvT)


def _diff_attention(dqT, dk, dvT, lq1, lk1, lq2, lk2, sg):
    seq = dk.shape[0]
    small = lambda a: pl.BlockSpec(a.shape, lambda h: (0, 0))
    out_shape = jax.ShapeDtypeStruct(dqT.shape, BF16)
    return pl.pallas_call(
        _diff_kernel,
        grid=(N_HEADS_DIFF,),
        in_specs=[
            _head_spec(dqT, 2 * HEAD_DIM),
            pl.BlockSpec((seq, LANES), lambda h: (0, h)),
            _head_spec(dvT, DIFF_V_ROWS),
            small(lq1), small(lk1), small(lq2), small(lk2), small(sg),
        ],
        out_specs=_head_spec(out_shape, DIFF_V_DIM),
        out_shape=out_shape,
        scratch_shapes=_attn_scratch(DIFF_BLOCK, 2, DIFF_V_ROWS),
        compiler_params=pltpu.CompilerParams(
            dimension_semantics=("arbitrary",), vmem_limit_bytes=VMEM_LIMIT_BYTES),
        name="diff_attn",
    )(dqT, dk, dvT, lq1, lk1, lq2, lk2, sg)


def _out_kernel(yf_ref, yd_ref, gf_ref, gd_ref, x_ref, w_ref, g_ref, o_ref):
    tn = (((0,), (0,)), ((), ()))
    gated = lambda y_ref, gate_ref: (y_ref[0].astype(F32) * gate_ref[0].astype(F32)).astype(BF16)
    y = lax.dot_general(gated(yf_ref, gf_ref), w_ref[:WIDTH, :], tn, preferred_element_type=F32)
    y = y + lax.dot_general(gated(yd_ref, gd_ref), w_ref[WIDTH:, :], tn, preferred_element_type=F32)
    h = x_ref[...] + y
    o_ref[...] = h * lax.rsqrt(jnp.mean(h * h, axis=-1, keepdims=True) + EPS) * g_ref[...]


def _out_project(yT_fox, yT_diff, fgT, dgT, x2d, w_out, final_g):
    seq = x2d.shape[0]
    nblk, _, tm = yT_fox.shape
    chunk = pl.BlockSpec((1, WIDTH, tm), lambda i: (i, 0, 0))
    return pl.pallas_call(
        _out_kernel,
        grid=(nblk,),
        in_specs=[
            chunk, chunk, chunk, chunk,
            pl.BlockSpec((tm, D_MODEL), lambda i: (i, 0)),
            pl.BlockSpec(w_out.shape, lambda i: (0, 0)),
            pl.BlockSpec((1, D_MODEL), lambda i: (0, 0)),
        ],
        out_specs=pl.BlockSpec((tm, D_MODEL), lambda i: (i, 0)),
        out_shape=jax.ShapeDtypeStruct((seq, D_MODEL), F32),
        compiler_params=pltpu.CompilerParams(
            dimension_semantics=("arbitrary",), vmem_limit_bytes=VMEM_LIMIT_BYTES),
        name="out_proj",
    )(yT_fox, yT_diff, fgT, dgT, x2d, w_out, final_g)


def _bias_selector():
    sel = np.zeros((N_BIAS_PIECES, LANES, N_HEADS_FOX * LANES), np.float32)
    for p in range(N_BIAS_PIECES):
        for h in range(N_HEADS_FOX):
            sel[p, h, h * LANES + HEAD_DIM + p] = 1.0
    return jnp.asarray(sel, BF16)


def kernel(x, norm_g, w_in, b_forget, lambda_q1, lambda_k1, lambda_q2, lambda_k2, subln_g, w_out, final_g):
    batch, seq, _ = x.shape
    assert batch == 1
    for blk in (FOX_BLOCK, DIFF_BLOCK):
        assert seq % blk == 0 and seq // blk >= 3 and blk % ROW_BLOCK == 0
    x2d = x[0]
    w = w_in[0]
    sizes = (WIDTH, WIDTH, WIDTH, WIDTH, N_HEADS_FOX, WIDTH, WIDTH, WIDTH, WIDTH)
    offs = np.concatenate([[0], np.cumsum(sizes)])
    g = {n: w[:, offs[j]:offs[j + 1]] for j, n in enumerate(("fq", "fk", "fv", "fg", "fz", "dq", "dk", "dv", "dg"))}
    wt = jnp.concatenate([g[n] for n in _T_GROUPS], axis=1).T.astype(BF16)
    wfk = jnp.pad(g["fk"].reshape(D_MODEL, N_HEADS_FOX, HEAD_DIM),
                  ((0, 0), (0, 0), (0, LANES - HEAD_DIM))).reshape(D_MODEL, N_HEADS_FOX * LANES).astype(BF16)
    wfz = jnp.pad(g["fz"], ((0, 0), (0, LANES - N_HEADS_FOX))).astype(BF16)
    bf = jnp.pad(b_forget[0].astype(F32), (0, LANES - N_HEADS_FOX)).reshape(1, LANES)
    invf = (ROPE_THETA ** (-jnp.arange(0, HEAD_DIM, 2, dtype=F32) / HEAD_DIM)).reshape(HEAD_DIM // 2, 1)

    fqT, fk, fvT, fgT, dqT, dk, dvT, dgT = _project(
        x2d, norm_g[0].reshape(1, D_MODEL), wt, wfk, wfz, bf, invf, _bias_selector())

    yT_fox = _fox_attention(fqT, fk, fvT)
    row = lambda a: a[0].astype(F32).reshape(1, HEAD_DIM)
    yT_diff = _diff_attention(dqT, dk, dvT, row(lambda_q1), row(lambda_k1), row(lambda_q2), row(lambda_k2),
                              subln_g[0].astype(F32).reshape(DIFF_V_DIM, 1))
    out = _out_project(yT_fox, yT_diff, fgT, dgT, x2d, w_out[0].astype(BF16), final_g.reshape(1, D_MODEL))
    return out[None]
```

```python
import math

import jax
import jax.numpy as jnp
import numpy as np
from jax import lax
from jax.experimental import pallas as pl
from jax.experimental.pallas import tpu as pltpu

D_MODEL = 1024
HEAD_DIM = 64
N_HEADS_FOX = 8
N_HEADS_DIFF = 4
N_MAPS_DIFF = 2 * N_HEADS_DIFF
DIFF_V_DIM = 2 * HEAD_DIM
WIDTH = 512
CHUNK = 64
ROPE_THETA = 10000.0
EPS = 1e-6
LAMBDA_INIT = 0.8 - 0.6 * math.exp(-0.3 * 0)
SCALE = HEAD_DIM ** -0.5
LOG2E = math.log2(math.e)
Q_SCALE = SCALE * LOG2E

LANES = 128
ROW_BLOCK = 512
FOX_BLOCK = 2048
DIFF_BLOCK = 1024
SUB_KEYS = 256
FOX_Q_TILE = 512
DIFF_Q_TILE = 256
N_BIAS_PIECES = 3
BF16_SUBLANES = 16
GATE_ROWS = BF16_SUBLANES
FOX_V_ROWS = HEAD_DIM + BF16_SUBLANES
DIFF_V_ROWS = DIFF_V_DIM + BF16_SUBLANES
VMEM_LIMIT_BYTES = 56 * 1024 * 1024

F32 = jnp.float32
BF16 = jnp.bfloat16

_T_GROUPS = ("fq", "fv", "fg", "dq", "dk", "dv", "dg")
_T_OFF = {name: i * WIDTH for i, name in enumerate(_T_GROUPS + ("fz",))}
_T_ROWS = _T_OFF["fz"] + GATE_ROWS


def _dot_nt(a, b):
    return lax.dot_general(a, b, (((1,), (1,)), ((), ())), preferred_element_type=F32)


def _dot(a, b):
    return jnp.dot(a, b, preferred_element_type=F32)


def _split_bf16(x):
    hi = x.astype(BF16)
    r = x - hi.astype(F32)
    mid = r.astype(BF16)
    lo = (r - mid.astype(F32)).astype(BF16)
    return hi, mid, lo


def _proj_kernel(x_ref, ng_ref, wt_ref, wfk_ref, bf_ref, invf_ref, sel_ref,
                 fqT_ref, fk_ref, fvT_ref, fgT_ref, dqT_ref, dk_ref, dvT_ref, dgT_ref,
                 carry_ref):
    i = pl.program_id(0)
    tm = x_ref.shape[0]

    @pl.when(i == 0)
    def _():
        carry_ref[...] = jnp.zeros_like(carry_ref)

    x = x_ref[...]
    u = x * lax.rsqrt(jnp.mean(x * x, axis=-1, keepdims=True) + EPS) * ng_ref[...]
    ub = u.astype(BF16)

    def proj_t(name, width=WIDTH):
        off = _T_OFF[name]
        return _dot_nt(wt_ref[off:off + width, :], ub)

    qT = proj_t("fq") * Q_SCALE
    row = lax.broadcasted_iota(jnp.int32, (LANES - HEAD_DIM, tm), 0)
    ones_rows = jnp.where(row < N_BIAS_PIECES, 1.0, 0.0).astype(BF16)
    for h in range(N_HEADS_FOX):
        fqT_ref[0, h * LANES:h * LANES + HEAD_DIM, :] = qT[h * HEAD_DIM:(h + 1) * HEAD_DIM, :].astype(BF16)
        fqT_ref[0, h * LANES + HEAD_DIM:(h + 1) * LANES, :] = ones_rows
    row = lax.broadcasted_iota(jnp.int32, (BF16_SUBLANES, tm), 0)
    denom_rows = jnp.where(row == 0, 1.0, 0.0).astype(BF16)
    vT = proj_t("fv").astype(BF16)
    for h in range(N_HEADS_FOX):
        fvT_ref[0, h * FOX_V_ROWS:h * FOX_V_ROWS + HEAD_DIM, :] = vT[h * HEAD_DIM:(h + 1) * HEAD_DIM, :]
        fvT_ref[0, h * FOX_V_ROWS + HEAD_DIM:(h + 1) * FOX_V_ROWS, :] = denom_rows
    fg = proj_t("fg")
    fgT_ref[0] = (fg * jax.nn.sigmoid(fg)).astype(BF16)

    zT = proj_t("fz", GATE_ROWS) + bf_ref[...]
    logf = jax.nn.log_sigmoid(zT)
    lane = lax.broadcasted_iota(jnp.int32, logf.shape, 1)
    shift = 1
    while shift < tm:
        logf = logf + jnp.where(lane >= shift, pltpu.roll(logf, shift, axis=1), 0.0)
        shift *= 2
    cum = logf + carry_ref[...]
    carry_ref[...] = cum[:, tm - 1:tm]
    pieces = [p.astype(F32) for p in _split_bf16(cum * (-LOG2E))]
    pieces.append(jnp.zeros((LANES - N_BIAS_PIECES * GATE_ROWS, tm), F32))
    pieces_t = jnp.concatenate(pieces, axis=0).T.astype(BF16)
    fk_ref[...] = (_dot(ub, wfk_ref[...]) + _dot(pieces_t, sel_ref[...])).astype(BF16)

    pos = (i * tm + lax.broadcasted_iota(jnp.int32, (1, tm), 1)).astype(F32)
    ang = pos * invf_ref[...]
    cos, sin = jnp.cos(ang), jnp.sin(ang)
    half = HEAD_DIM // 2

    def rope_t(xt):
        x1, x2 = xt[:half, :], xt[half:, :]
        return jnp.concatenate([x1 * cos - x2 * sin, x2 * cos + x1 * sin], axis=0)

    qT = proj_t("dq")
    for m in range(N_MAPS_DIFF):
        rows = slice(m * HEAD_DIM, (m + 1) * HEAD_DIM)
        dqT_ref[0, rows, :] = (rope_t(qT[rows, :]) * Q_SCALE).astype(BF16)
    kT = proj_t("dk")
    kT = jnp.concatenate([rope_t(kT[m * HEAD_DIM:(m + 1) * HEAD_DIM, :]) for m in range(N_MAPS_DIFF)], axis=0)
    dk_ref[...] = kT.T.astype(BF16)
    vT = proj_t("dv").astype(BF16)
    for h in range(N_HEADS_DIFF):
        dvT_ref[0, h * DIFF_V_ROWS:h * DIFF_V_ROWS + DIFF_V_DIM, :] = vT[h * DIFF_V_DIM:(h + 1) * DIFF_V_DIM, :]
        dvT_ref[0, h * DIFF_V_ROWS + DIFF_V_DIM:(h + 1) * DIFF_V_ROWS, :] = denom_rows
    dg = proj_t("dg")
    dgT_ref[0] = (dg * jax.nn.sigmoid(dg)).astype(BF16)


def _project(x2d, norm_g, wt, wfk, bf, invf, sel):
    seq = x2d.shape[0]
    tm = ROW_BLOCK
    nblk = seq // tm
    const = lambda shape: pl.BlockSpec(shape, lambda i: (0,) * len(shape))
    chunk_t = lambda rows: pl.BlockSpec((1, rows, tm), lambda i: (i, 0, 0))
    chunked = lambda rows: jax.ShapeDtypeStruct((nblk, rows, tm), BF16)
    out_shape = (
        chunked(N_HEADS_FOX * LANES),
        jax.ShapeDtypeStruct((seq, N_HEADS_FOX * LANES), BF16),
        chunked(N_HEADS_FOX * FOX_V_ROWS),
        chunked(WIDTH),
        chunked(WIDTH),
        jax.ShapeDtypeStruct((seq, WIDTH), BF16),
        chunked(N_HEADS_DIFF * DIFF_V_ROWS),
        chunked(WIDTH),
    )
    return pl.pallas_call(
        _proj_kernel,
        grid=(nblk,),
        in_specs=[
            pl.BlockSpec((tm, D_MODEL), lambda i: (i, 0)),
            const((1, D_MODEL)),
            const(wt.shape), const(wfk.shape), const(bf.shape),
            const(invf.shape), const(sel.shape),
        ],
        out_specs=(
            chunk_t(N_HEADS_FOX * LANES),
            pl.BlockSpec((tm, N_HEADS_FOX * LANES), lambda i: (i, 0)),
            chunk_t(N_HEADS_FOX * FOX_V_ROWS), chunk_t(WIDTH),
            chunk_t(WIDTH),
            pl.BlockSpec((tm, WIDTH), lambda i: (i, 0)),
            chunk_t(N_HEADS_DIFF * DIFF_V_ROWS), chunk_t(WIDTH),
        ),
        out_shape=out_shape,
        scratch_shapes=[pltpu.VMEM((GATE_ROWS, 1), F32)],
        compiler_params=pltpu.CompilerParams(
            dimension_semantics=("arbitrary",), vmem_limit_bytes=VMEM_LIMIT_BYTES),
        name="proj",
    )(x2d, norm_g, wt, wfk, bf, invf, sel)


def _attn_head(load_rhs, finalize, mask_fn, k_ref, vT_ref, rhs_ref, s_ref, mblk_ref, m_ref, acc_ref,
               *, blk, n_maps, q_tile):
    nq = n_maps * blk
    v_rows, v_chunk = vT_ref.shape[1:]
    n_sub = blk // SUB_KEYS
    n_qblocks = k_ref.shape[0] // blk
    neg_inf = lambda cols: jnp.full((1, cols), -jnp.inf, F32)

    def tile_kind(diag, k0, q0):
        if not diag or k0 + SUB_KEYS <= q0:
            return "full"
        return "skip" if k0 >= q0 + q_tile else "mask"

    def step(consume, produce):
        if consume is not None:
            m_old = m_ref[...]
            m_new = jnp.maximum(m_old, mblk_ref[...])
            m_ref[...] = m_new
            alpha = jnp.exp2(m_old - m_new)
        for cq in range(nq // q_tile):
            cols = slice(cq * q_tile, (cq + 1) * q_tile)
            q0 = (cq * q_tile) % blk
            pv, m_next = None, None
            for j in range(n_sub):
                k0 = j * SUB_KEYS
                rows = slice(k0, k0 + SUB_KEYS)
                p_kind = tile_kind(produce[1], k0, q0) if produce is not None else "skip"
                c_kind = tile_kind(consume[1], k0, q0) if consume is not None else "skip"
                if p_kind != "skip":
                    row0 = produce[0] * blk + k0
                    if not isinstance(row0, int):
                        row0 = pl.multiple_of(row0, SUB_KEYS)
                    s = _dot(k_ref[pl.ds(row0, SUB_KEYS), :], rhs_ref[:, cols])
                    if p_kind == "mask":
                        key_pos = k0 + lax.broadcasted_iota(jnp.int32, s.shape, 0)
                        query_pos = q0 + lax.broadcasted_iota(jnp.int32, s.shape, 1)
                        s = jnp.where(mask_fn(key_pos, query_pos), s, -jnp.inf)
                if c_kind != "skip":
                    c, off = divmod(k0, v_chunk)
                    v_sub = vT_ref[consume[0] * (blk // v_chunk) + c, :, off:off + SUB_KEYS]
                    d = _dot(v_sub, jnp.exp2(s_ref[rows, cols] - m_new[:, cols]).astype(BF16))
                    pv = d if pv is None else pv + d
                if p_kind != "skip":
                    s_ref[rows, cols] = s
                    m_j = jnp.max(s, axis=0, keepdims=True)
                    m_next = m_j if m_next is None else jnp.maximum(m_next, m_j)
            if consume is not None:
                acc_ref[:, cols] = alpha[:, cols] * acc_ref[:, cols] + pv
            if produce is not None:
                mblk_ref[:, cols] = m_next

    def reset():
        m_ref[...] = neg_inf(nq)
        acc_ref[...] = jnp.zeros_like(acc_ref)

    def rest_of_query_block(q, produce_after):
        step((q, True), (0, False))

        def two_steps(i, _):
            kb = 1 + 2 * i
            step((kb - 1, False), (kb, False))
            step((kb, False), (kb + 1, False))

        n_pairs = (q - 1) // 2
        lax.fori_loop(0, n_pairs, two_steps, None)
        lax.fori_loop(1 + 2 * n_pairs, q, lambda kb, _: step((kb - 1, False), (kb, False)), None)
        if produce_after is not None:
            load_rhs(q + 1)
        step((q - 1, False), produce_after)
        finalize(q)

    reset()
    load_rhs(0)
    step(None, (0, True))
    load_rhs(1)
    step((0, True), (1, True))
    finalize(0)
    reset()

    def middle(q, _):
        rest_of_query_block(q, (q + 1, True))
        reset()

    lax.fori_loop(1, n_qblocks - 1, middle, None)
    rest_of_query_block(n_qblocks - 1, None)


def _fox_kernel(qT_ref, k_ref, vT_ref, o_ref, rhs_ref, s_ref, mblk_ref, m_ref, acc_ref):
    blk = s_ref.shape[0]
    chunk = qT_ref.shape[2]
    per_block = blk // chunk

    def load_rhs(q):
        for c in range(per_block):
            rhs_ref[:, c * chunk:(c + 1) * chunk] = qT_ref[q * per_block + c]

    def finalize(q):
        acc = acc_ref[...]
        o = acc[:HEAD_DIM, :] / acc[HEAD_DIM:HEAD_DIM + 1, :]
        for c in range(per_block):
            o_ref[q * per_block + c] = o[:, c * chunk:(c + 1) * chunk].astype(o_ref.dtype)

    def mask_fn(key_pos, query_pos):
        return key_pos <= query_pos

    _attn_head(load_rhs, finalize, mask_fn, k_ref, vT_ref, rhs_ref, s_ref, mblk_ref, m_ref, acc_ref,
               blk=blk, n_maps=1, q_tile=FOX_Q_TILE)


def _diff_kernel(qT_ref, k_ref, vT_ref, lq1_ref, lk1_ref, lq2_ref, lk2_ref, sg_ref, o_ref,
                 rhs_ref, s_ref, mblk_ref, m_ref, acc_ref):
    blk = s_ref.shape[0]
    chunk = qT_ref.shape[2]
    per_block = blk // chunk

    def load_rhs(q):
        zeros = jnp.zeros((HEAD_DIM, chunk), BF16)
        for c in range(per_block):
            q12 = qT_ref[q * per_block + c]
            rhs_ref[:, c * chunk:(c + 1) * chunk] = jnp.concatenate([q12[:HEAD_DIM], zeros], axis=0)
            rhs_ref[:, blk + c * chunk:blk + (c + 1) * chunk] = jnp.concatenate([zeros, q12[HEAD_DIM:]], axis=0)

    def finalize(q):
        acc = acc_ref[...]
        o = acc[:DIFF_V_DIM, :] / acc[DIFF_V_DIM:DIFF_V_DIM + 1, :]
        lam = (jnp.exp(jnp.sum(lq1_ref[...] * lk1_ref[...], axis=-1, keepdims=True))
               - jnp.exp(jnp.sum(lq2_ref[...] * lk2_ref[...], axis=-1, keepdims=True)) + LAMBDA_INIT)
        y = o[:, :blk] - lam * o[:, blk:]
        y = y * lax.rsqrt(jnp.mean(y * y, axis=0, keepdims=True) + EPS) * sg_ref[...]
        y = y * (1.0 - LAMBDA_INIT)
        for c in range(per_block):
            o_ref[q * per_block + c] = y[:, c * chunk:(c + 1) * chunk].astype(o_ref.dtype)

    def mask_fn(key_pos, query_pos):
        return key_pos // CHUNK <= query_pos // CHUNK

    _attn_head(load_rhs, finalize, mask_fn, k_ref, vT_ref, rhs_ref, s_ref, mblk_ref, m_ref, acc_ref,
               blk=blk, n_maps=2, q_tile=DIFF_Q_TILE)


def _head_spec(arr, rows):
    return pl.BlockSpec((arr.shape[0], rows, arr.shape[2]), lambda h: (0, h, 0))


def _attn_scratch(blk, n_maps, v_rows):
    nq = n_maps * blk
    return [pltpu.VMEM((LANES, nq), BF16), pltpu.VMEM((blk, nq), F32),
            pltpu.VMEM((1, nq), F32), pltpu.VMEM((1, nq), F32), pltpu.VMEM((v_rows, nq), F32)]


def _fox_attention(fqT, fk, fvT):
    seq = fk.shape[0]
    out_shape = jax.ShapeDtypeStruct((fqT.shape[0], WIDTH, fqT.shape[2]), BF16)
    return pl.pallas_call(
        _fox_kernel,
        grid=(N_HEADS_FOX,),
        in_specs=[
            _head_spec(fqT, LANES),
            pl.BlockSpec((seq, LANES), lambda h: (0, h)),
            _head_spec(fvT, FOX_V_ROWS),
        ],
        out_specs=_head_spec(out_shape, HEAD_DIM),
        out_shape=out_shape,
        scratch_shapes=_attn_scratch(FOX_BLOCK, 1, FOX_V_ROWS),
        compiler_params=pltpu.CompilerParams(
            dimension_semantics=("arbitrary",), vmem_limit_bytes=VMEM_LIMIT_BYTES),
        name="fox_attn",
    )(fqT, fk, fvT)


def _diff_attention(dqT, dk, dvT, lq1, lk1, lq2, lk2, sg):
    seq = dk.shape[0]
    small = lambda a: pl.BlockSpec(a.shape, lambda h: (0, 0))
    out_shape = jax.ShapeDtypeStruct(dqT.shape, BF16)
    return pl.pallas_call(
        _diff_kernel,
        grid=(N_HEADS_DIFF,),
        in_specs=[
            _head_spec(dqT, 2 * HEAD_DIM),
            pl.BlockSpec((seq, LANES), lambda h: (0, h)),
            _head_spec(dvT, DIFF_V_ROWS),
            small(lq1), small(lk1), small(lq2), small(lk2), small(sg),
        ],
        out_specs=_head_spec(out_shape, DIFF_V_DIM),
        out_shape=out_shape,
        scratch_shapes=_attn_scratch(DIFF_BLOCK, 2, DIFF_V_ROWS),
        compiler_params=pltpu.CompilerParams(
            dimension_semantics=("arbitrary",), vmem_limit_bytes=VMEM_LIMIT_BYTES),
        name="diff_attn",
    )(dqT, dk, dvT, lq1, lk1, lq2, lk2, sg)


def _out_kernel(yf_ref, yd_ref, gf_ref, gd_ref, x_ref, w_ref, g_ref, o_ref):
    tn = (((0,), (0,)), ((), ()))
    gated = lambda y_ref, gate_ref: (y_ref[0].astype(F32) * gate_ref[0].astype(F32)).astype(BF16)
    y = lax.dot_general(gated(yf_ref, gf_ref), w_ref[:WIDTH, :], tn, preferred_element_type=F32)
    y = y + lax.dot_general(gated(yd_ref, gd_ref), w_ref[WIDTH:, :], tn, preferred_element_type=F32)
    h = x_ref[...] + y
    o_ref[...] = h * lax.rsqrt(jnp.mean(h * h, axis=-1, keepdims=True) + EPS) * g_ref[...]


def _out_project(yT_fox, yT_diff, fgT, dgT, x2d, w_out, final_g):
    seq = x2d.shape[0]
    nblk, _, tm = yT_fox.shape
    chunk = pl.BlockSpec((1, WIDTH, tm), lambda i: (i, 0, 0))
    return pl.pallas_call(
        _out_kernel,
        grid=(nblk,),
        in_specs=[
            chunk, chunk, chunk, chunk,
            pl.BlockSpec((tm, D_MODEL), lambda i: (i, 0)),
            pl.BlockSpec(w_out.shape, lambda i: (0, 0)),
            pl.BlockSpec((1, D_MODEL), lambda i: (0, 0)),
        ],
        out_specs=pl.BlockSpec((tm, D_MODEL), lambda i: (i, 0)),
        out_shape=jax.ShapeDtypeStruct((seq, D_MODEL), F32),
        compiler_params=pltpu.CompilerParams(
            dimension_semantics=("arbitrary",), vmem_limit_bytes=VMEM_LIMIT_BYTES),
        name="out_proj",
    )(yT_fox, yT_diff, fgT, dgT, x2d, w_out, final_g)


_COL_SIZES = (("fq", WIDTH), ("fk", WIDTH), ("fv", WIDTH), ("fg", WIDTH), ("fz", N_HEADS_FOX),
              ("dq", WIDTH), ("dk", WIDTH), ("dv", WIDTH), ("dg", WIDTH))
_COL_OFF = {name: sum(size for _, size in _COL_SIZES[:j]) for j, (name, _) in enumerate(_COL_SIZES)}


def _prep_kernel(w_ref, place_ref, wt_ref, wfk_ref):
    n_cols = w_ref.shape[1]

    def group(name):
        off = _COL_OFF[name]
        lo = off // LANES * LANES
        hi = min(lo + WIDTH + (LANES if off > lo else 0), n_cols)
        return w_ref[:, lo:hi][:, off - lo:off - lo + WIDTH]

    for gi, name in enumerate(_T_GROUPS):
        wt_ref[gi * WIDTH:(gi + 1) * WIDTH, :] = group(name).T.astype(BF16)
    wfk_ref[...] = _dot(group("fk").astype(BF16), place_ref[...]).astype(BF16)
    off = _COL_OFF["fz"]
    lane = lax.broadcasted_iota(jnp.int32, (w_ref.shape[0], LANES), 1)
    fz = jnp.where(lane < N_HEADS_FOX, w_ref[:, off:off + LANES], 0.0)
    wt_ref[_T_OFF["fz"]:_T_ROWS, :] = fz.T[:GATE_ROWS, :].astype(BF16)


def _prepare_weights(w):
    d, n_cols = w.shape
    place = np.zeros((WIDTH, N_HEADS_FOX * LANES), np.float32)
    for h in range(N_HEADS_FOX):
        place[h * HEAD_DIM + np.arange(HEAD_DIM), h * LANES + np.arange(HEAD_DIM)] = 1.0
    place = jnp.asarray(place, BF16)
    full = lambda shape: pl.BlockSpec(shape, lambda i: (0,) * len(shape))
    out_shape = (jax.ShapeDtypeStruct((_T_ROWS, d), BF16),
                 jax.ShapeDtypeStruct((d, N_HEADS_FOX * LANES), BF16))
    return pl.pallas_call(
        _prep_kernel,
        grid=(1,),
        in_specs=[pl.BlockSpec((d, n_cols), lambda i: (0, 0), pipeline_mode=pl.Buffered(1)), full(place.shape)],
        out_specs=tuple(full(o.shape) for o in out_shape),
        out_shape=out_shape,
        compiler_params=pltpu.CompilerParams(
            dimension_semantics=("arbitrary",), vmem_limit_bytes=VMEM_LIMIT_BYTES),
        name="prep_weights",
    )(w, place)


def _bias_selector():
    sel = np.zeros((LANES, N_HEADS_FOX * LANES), np.float32)
    for p in range(N_BIAS_PIECES):
        for h in range(N_HEADS_FOX):
            sel[p * GATE_ROWS + h, h * LANES + HEAD_DIM + p] = 1.0
    return jnp.asarray(sel, BF16)


def kernel(x, norm_g, w_in, b_forget, lambda_q1, lambda_k1, lambda_q2, lambda_k2, subln_g, w_out, final_g):
    batch, seq, _ = x.shape
    assert batch == 1
    for blk in (FOX_BLOCK, DIFF_BLOCK):
        assert seq % blk == 0 and seq // blk >= 3 and blk % ROW_BLOCK == 0
    x2d = x[0]
    wt, wfk = _prepare_weights(w_in[0])
    bf = jnp.pad(b_forget[0].astype(F32), (0, GATE_ROWS - N_HEADS_FOX)).reshape(GATE_ROWS, 1)
    invf = (ROPE_THETA ** (-jnp.arange(0, HEAD_DIM, 2, dtype=F32) / HEAD_DIM)).reshape(HEAD_DIM // 2, 1)

    fqT, fk, fvT, fgT, dqT, dk, dvT, dgT = _project(
        x2d, norm_g[0].reshape(1, D_MODEL), wt, wfk, bf, invf, _bias_selector())

    yT_fox = _fox_attention(fqT, fk, fvT)
    row = lambda a: a[0].astype(F32).reshape(1, HEAD_DIM)
    yT_diff = _diff_attention(dqT, dk, dvT, row(lambda_q1), row(lambda_k1), row(lambda_q2), row(lambda_k2),
                              subln_g[0].astype(F32).reshape(DIFF_V_DIM, 1))
    out = _out_project(yT_fox, yT_diff, fgT, dgT, x2d, w_out[0].astype(BF16), final_g.reshape(1, D_MODEL))
    return out[None]
```

```python
import math

import jax
import jax.numpy as jnp
import numpy as np
from jax import lax
from jax.experimental import pallas as pl
from jax.experimental.pallas import tpu as pltpu

D_MODEL = 1024
HEAD_DIM = 64
N_HEADS_FOX = 8
N_HEADS_DIFF = 4
N_MAPS_DIFF = 2 * N_HEADS_DIFF
DIFF_V_DIM = 2 * HEAD_DIM
WIDTH = 512
CHUNK = 64
ROPE_THETA = 10000.0
EPS = 1e-6
LAMBDA_INIT = 0.8 - 0.6 * math.exp(-0.3 * 0)
SCALE = HEAD_DIM ** -0.5
LOG2E = math.log2(math.e)
Q_SCALE = SCALE * LOG2E

LANES = 128
ROW_BLOCK = 512
FOX_BLOCK = 2048
DIFF_BLOCK = 1024
SUB_KEYS = 256
FOX_Q_TILE = 256
DIFF_Q_TILE = 256
N_BIAS_PIECES = 3
BF16_SUBLANES = 16
GATE_ROWS = BF16_SUBLANES
FOX_V_ROWS = HEAD_DIM + BF16_SUBLANES
DIFF_V_ROWS = DIFF_V_DIM + BF16_SUBLANES
VMEM_LIMIT_BYTES = 56 * 1024 * 1024

F32 = jnp.float32
BF16 = jnp.bfloat16

_T_GROUPS = ("fq", "fv", "fg", "dq", "dk", "dv", "dg")
_T_OFF = {name: i * WIDTH for i, name in enumerate(_T_GROUPS + ("fz",))}
_T_ROWS = _T_OFF["fz"] + GATE_ROWS


def _dot_nt(a, b):
    return lax.dot_general(a, b, (((1,), (1,)), ((), ())), preferred_element_type=F32)


def _dot(a, b):
    return jnp.dot(a, b, preferred_element_type=F32)


def _split_bf16(x):
    hi = x.astype(BF16)
    r = x - hi.astype(F32)
    mid = r.astype(BF16)
    lo = (r - mid.astype(F32)).astype(BF16)
    return hi, mid, lo


def _proj_kernel(x_ref, ng_ref, wt_ref, wfk_ref, bf_ref, invf_ref, sel_ref,
                 fqT_ref, fk_ref, fvT_ref, fgT_ref, dqT_ref, dk_ref, dvT_ref, dgT_ref,
                 carry_ref):
    i = pl.program_id(0)
    tm = x_ref.shape[0]

    @pl.when(i == 0)
    def _():
        carry_ref[...] = jnp.zeros_like(carry_ref)

    x = x_ref[...]
    u = x * lax.rsqrt(jnp.mean(x * x, axis=-1, keepdims=True) + EPS) * ng_ref[...]
    ub = u.astype(BF16)

    def proj_t(name, width=WIDTH):
        off = _T_OFF[name]
        return _dot_nt(wt_ref[off:off + width, :], ub)


    zT = proj_t("fz", GATE_ROWS) + bf_ref[...]
    logf = jax.nn.log_sigmoid(zT)
    lane = lax.broadcasted_iota(jnp.int32, logf.shape, 1)
    shift = 1
    while shift < tm:
        logf = logf + jnp.where(lane >= shift, pltpu.roll(logf, shift, axis=1), 0.0)
        shift *= 2
    cum = logf + carry_ref[...]
    carry_ref[...] = cum[:, tm - 1:tm]
    pieces = [p.astype(F32) for p in _split_bf16(cum * (-LOG2E))]
    pieces.append(jnp.zeros((LANES - N_BIAS_PIECES * GATE_ROWS, tm), F32))
    pieces_t = jnp.concatenate(pieces, axis=0).T.astype(BF16)

    fg = proj_t("fg")
    fgT_ref[0] = (fg * jax.nn.sigmoid(fg)).astype(BF16)
    dg = proj_t("dg")
    dgT_ref[0] = (dg * jax.nn.sigmoid(dg)).astype(BF16)

    pos = (i * tm + lax.broadcasted_iota(jnp.int32, (1, tm), 1)).astype(F32)
    ang = pos * invf_ref[...]
    cos, sin = jnp.cos(ang), jnp.sin(ang)
    half = HEAD_DIM // 2

    def rope_t(xt):
        x1, x2 = xt[:half, :], xt[half:, :]
        return jnp.concatenate([x1 * cos - x2 * sin, x2 * cos + x1 * sin], axis=0)

    qT = proj_t("dq")
    for m in range(N_MAPS_DIFF):
        rows = slice(m * HEAD_DIM, (m + 1) * HEAD_DIM)
        dqT_ref[0, rows, :] = (rope_t(qT[rows, :]) * Q_SCALE).astype(BF16)
    kT = proj_t("dk")
    kT = jnp.concatenate([rope_t(kT[m * HEAD_DIM:(m + 1) * HEAD_DIM, :]) for m in range(N_MAPS_DIFF)], axis=0)
    dk_ref[...] = kT.T.astype(BF16)

    qT = proj_t("fq") * Q_SCALE
    row = lax.broadcasted_iota(jnp.int32, (LANES - HEAD_DIM, tm), 0)
    ones_rows = jnp.where(row < N_BIAS_PIECES, 1.0, 0.0).astype(BF16)
    for h in range(N_HEADS_FOX):
        fqT_ref[0, h * LANES:h * LANES + HEAD_DIM, :] = qT[h * HEAD_DIM:(h + 1) * HEAD_DIM, :].astype(BF16)
        fqT_ref[0, h * LANES + HEAD_DIM:(h + 1) * LANES, :] = ones_rows

    row = lax.broadcasted_iota(jnp.int32, (BF16_SUBLANES, tm), 0)
    denom_rows = jnp.where(row == 0, 1.0, 0.0).astype(BF16)
    vT = proj_t("fv").astype(BF16)
    for h in range(N_HEADS_FOX):
        fvT_ref[0, h * FOX_V_ROWS:h * FOX_V_ROWS + HEAD_DIM, :] = vT[h * HEAD_DIM:(h + 1) * HEAD_DIM, :]
        fvT_ref[0, h * FOX_V_ROWS + HEAD_DIM:(h + 1) * FOX_V_ROWS, :] = denom_rows
    vT = proj_t("dv").astype(BF16)
    for h in range(N_HEADS_DIFF):
        dvT_ref[0, h * DIFF_V_ROWS:h * DIFF_V_ROWS + DIFF_V_DIM, :] = vT[h * DIFF_V_DIM:(h + 1) * DIFF_V_DIM, :]
        dvT_ref[0, h * DIFF_V_ROWS + DIFF_V_DIM:(h + 1) * DIFF_V_ROWS, :] = denom_rows

    fk_ref[...] = (_dot(ub, wfk_ref[...]) + _dot(pieces_t, sel_ref[...])).astype(BF16)


def _project(x2d, norm_g, wt, wfk, bf, invf, sel):
    seq = x2d.shape[0]
    tm = ROW_BLOCK
    nblk = seq // tm
    const = lambda shape: pl.BlockSpec(shape, lambda i: (0,) * len(shape))
    chunk_t = lambda rows: pl.BlockSpec((1, rows, tm), lambda i: (i, 0, 0))
    chunked = lambda rows: jax.ShapeDtypeStruct((nblk, rows, tm), BF16)
    out_shape = (
        chunked(N_HEADS_FOX * LANES),
        jax.ShapeDtypeStruct((seq, N_HEADS_FOX * LANES), BF16),
        chunked(N_HEADS_FOX * FOX_V_ROWS),
        chunked(WIDTH),
        chunked(WIDTH),
        jax.ShapeDtypeStruct((seq, WIDTH), BF16),
        chunked(N_HEADS_DIFF * DIFF_V_ROWS),
        chunked(WIDTH),
    )
    return pl.pallas_call(
        _proj_kernel,
        grid=(nblk,),
        in_specs=[
            pl.BlockSpec((tm, D_MODEL), lambda i: (i, 0)),
            const((1, D_MODEL)),
            const(wt.shape), const(wfk.shape), const(bf.shape),
            const(invf.shape), const(sel.shape),
        ],
        out_specs=(
            chunk_t(N_HEADS_FOX * LANES),
            pl.BlockSpec((tm, N_HEADS_FOX * LANES), lambda i: (i, 0)),
            chunk_t(N_HEADS_FOX * FOX_V_ROWS), chunk_t(WIDTH),
            chunk_t(WIDTH),
            pl.BlockSpec((tm, WIDTH), lambda i: (i, 0)),
            chunk_t(N_HEADS_DIFF * DIFF_V_ROWS), chunk_t(WIDTH),
        ),
        out_shape=out_shape,
        scratch_shapes=[pltpu.VMEM((GATE_ROWS, 1), F32)],
        compiler_params=pltpu.CompilerParams(
            dimension_semantics=("arbitrary",), vmem_limit_bytes=VMEM_LIMIT_BYTES),
        name="proj",
    )(x2d, norm_g, wt, wfk, bf, invf, sel)


def _attn_head(load_rhs, finalize, mask_fn, k_ref, vT_ref, rhs_ref, s_ref, mblk_ref, m_ref, acc_ref,
               *, blk, n_maps, q_tile):
    nq = n_maps * blk
    v_rows, v_chunk = vT_ref.shape[1:]
    n_sub = blk // SUB_KEYS
    n_qblocks = k_ref.shape[0] // blk
    neg_inf = lambda cols: jnp.full((1, cols), -jnp.inf, F32)

    def tile_kind(diag, k0, q0):
        if not diag or k0 + SUB_KEYS <= q0:
            return "full"
        return "skip" if k0 >= q0 + q_tile else "mask"

    def step(consume, produce):
        if consume is not None:
            m_old = m_ref[...]
            m_new = jnp.maximum(m_old, mblk_ref[...])
            m_ref[...] = m_new
            alpha = jnp.exp2(m_old - m_new)
        for cq in range(nq // q_tile):
            cols = slice(cq * q_tile, (cq + 1) * q_tile)
            q0 = (cq * q_tile) % blk
            pv, m_next = None, None
            for j in range(n_sub):
                k0 = j * SUB_KEYS
                rows = slice(k0, k0 + SUB_KEYS)
                p_kind = tile_kind(produce[1], k0, q0) if produce is not None else "skip"
                c_kind = tile_kind(consume[1], k0, q0) if consume is not None else "skip"
                if p_kind != "skip":
                    row0 = produce[0] * blk + k0
                    if not isinstance(row0, int):
                        row0 = pl.multiple_of(row0, SUB_KEYS)
                    s = _dot(k_ref[pl.ds(row0, SUB_KEYS), :], rhs_ref[:, cols])
                    if p_kind == "mask":
                        key_pos = k0 + lax.broadcasted_iota(jnp.int32, s.shape, 0)
                        query_pos = q0 + lax.broadcasted_iota(jnp.int32, s.shape, 1)
                        s = jnp.where(mask_fn(key_pos, query_pos), s, -jnp.inf)
                if c_kind != "skip":
                    c, off = divmod(k0, v_chunk)
                    v_sub = vT_ref[consume[0] * (blk // v_chunk) + c, :, off:off + SUB_KEYS]
                    d = _dot(v_sub, jnp.exp2(s_ref[rows, cols] - m_new[:, cols]).astype(BF16))
                    pv = d if pv is None else pv + d
                if p_kind != "skip":
                    s_ref[rows, cols] = s
                    m_j = jnp.max(s, axis=0, keepdims=True)
                    m_next = m_j if m_next is None else jnp.maximum(m_next, m_j)
            if consume is not None:
                acc_ref[:, cols] = alpha[:, cols] * acc_ref[:, cols] + pv
            if produce is not None:
                mblk_ref[:, cols] = m_next

    def reset():
        m_ref[...] = neg_inf(nq)
        acc_ref[...] = jnp.zeros_like(acc_ref)

    def rest_of_query_block(q, produce_after):
        step((q, True), (0, False))

        def two_steps(i, _):
            kb = 1 + 2 * i
            step((kb - 1, False), (kb, False))
            step((kb, False), (kb + 1, False))

        n_pairs = (q - 1) // 2
        lax.fori_loop(0, n_pairs, two_steps, None)
        lax.fori_loop(1 + 2 * n_pairs, q, lambda kb, _: step((kb - 1, False), (kb, False)), None)
        if produce_after is not None:
            load_rhs(q + 1)
        step((q - 1, False), produce_after)
        finalize(q)

    reset()
    load_rhs(0)
    step(None, (0, True))
    load_rhs(1)
    step((0, True), (1, True))
    finalize(0)
    reset()

    def middle(q, _):
        rest_of_query_block(q, (q + 1, True))
        reset()

    lax.fori_loop(1, n_qblocks - 1, middle, None)
    rest_of_query_block(n_qblocks - 1, None)


def _fox_kernel(qT_ref, k_ref, vT_ref, o_ref, rhs_ref, s_ref, mblk_ref, m_ref, acc_ref):
    blk = s_ref.shape[0]
    chunk = qT_ref.shape[2]
    per_block = blk // chunk

    def load_rhs(q):
        for c in range(per_block):
            rhs_ref[:, c * chunk:(c + 1) * chunk] = qT_ref[q * per_block + c]

    def finalize(q):
        acc = acc_ref[...]
        o = acc[:HEAD_DIM, :] / acc[HEAD_DIM:HEAD_DIM + 1, :]
        for c in range(per_block):
            o_ref[q * per_block + c] = o[:, c * chunk:(c + 1) * chunk].astype(o_ref.dtype)

    def mask_fn(key_pos, query_pos):
        return key_pos <= query_pos

    _attn_head(load_rhs, finalize, mask_fn, k_ref, vT_ref, rhs_ref, s_ref, mblk_ref, m_ref, acc_ref,
               blk=blk, n_maps=1, q_tile=FOX_Q_TILE)


def _diff_kernel(qT_ref, k_ref, vT_ref, lq1_ref, lk1_ref, lq2_ref, lk2_ref, sg_ref, o_ref,
                 rhs_ref, s_ref, mblk_ref, m_ref, acc_ref):
    blk = s_ref.shape[0]
    chunk = qT_ref.shape[2]
    per_block = blk // chunk

    def load_rhs(q):
        zeros = jnp.zeros((HEAD_DIM, chunk), BF16)
        for c in range(per_block):
            q12 = qT_ref[q * per_block + c]
            rhs_ref[:, c * chunk:(c + 1) * chunk] = jnp.concatenate([q12[:HEAD_DIM], zeros], axis=0)
            rhs_ref[:, blk + c * chunk:blk + (c + 1) * chunk] = jnp.concatenate([zeros, q12[HEAD_DIM:]], axis=0)

    def finalize(q):
        acc = acc_ref[...]
        o = acc[:DIFF_V_DIM, :] / acc[DIFF_V_DIM:DIFF_V_DIM + 1, :]
        lam = (jnp.exp(jnp.sum(lq1_ref[...] * lk1_ref[...], axis=-1, keepdims=True))
               - jnp.exp(jnp.sum(lq2_ref[...] * lk2_ref[...], axis=-1, keepdims=True)) + LAMBDA_INIT)
        y = o[:, :blk] - lam * o[:, blk:]
        y = y * lax.rsqrt(jnp.mean(y * y, axis=0, keepdims=True) + EPS) * sg_ref[...]
        y = y * (1.0 - LAMBDA_INIT)
        for c in range(per_block):
            o_ref[q * per_block + c] = y[:, c * chunk:(c + 1) * chunk].astype(o_ref.dtype)

    def mask_fn(key_pos, query_pos):
        return key_pos // CHUNK <= query_pos // CHUNK

    _attn_head(load_rhs, finalize, mask_fn, k_ref, vT_ref, rhs_ref, s_ref, mblk_ref, m_ref, acc_ref,
               blk=blk, n_maps=2, q_tile=DIFF_Q_TILE)


def _head_spec(arr, rows):
    return pl.BlockSpec((arr.shape[0], rows, arr.shape[2]), lambda h: (0, h, 0))


def _attn_scratch(blk, n_maps, v_rows):
    nq = n_maps * blk
    return [pltpu.VMEM((LANES, nq), BF16), pltpu.VMEM((blk, nq), F32),
            pltpu.VMEM((1, nq), F32), pltpu.VMEM((1, nq), F32), pltpu.VMEM((v_rows, nq), F32)]


def _fox_attention(fqT, fk, fvT):
    seq = fk.shape[0]
    out_shape = jax.ShapeDtypeStruct((fqT.shape[0], WIDTH, fqT.shape[2]), BF16)
    return pl.pallas_call(
        _fox_kernel,
        grid=(N_HEADS_FOX,),
        in_specs=[
            _head_spec(fqT, LANES),
            pl.BlockSpec((seq, LANES), lambda h: (0, h)),
            _head_spec(fvT, FOX_V_ROWS),
        ],
        out_specs=_head_spec(out_shape, HEAD_DIM),
        out_shape=out_shape,
        scratch_shapes=_attn_scratch(FOX_BLOCK, 1, FOX_V_ROWS),
        compiler_params=pltpu.CompilerParams(
            dimension_semantics=("arbitrary",), vmem_limit_bytes=VMEM_LIMIT_BYTES),
        name="fox_attn",
    )(fqT, fk, fvT)


def _diff_attention(dqT, dk, dvT, lq1, lk1, lq2, lk2, sg):
    seq = dk.shape[0]
    small = lambda a: pl.BlockSpec(a.shape, lambda h: (0, 0))
    out_shape = jax.ShapeDtypeStruct(dqT.shape, BF16)
    return pl.pallas_call(
        _diff_kernel,
        grid=(N_HEADS_DIFF,),
        in_specs=[
            _head_spec(dqT, 2 * HEAD_DIM),
            pl.BlockSpec((seq, LANES), lambda h: (0, h)),
            _head_spec(dvT, DIFF_V_ROWS),
            small(lq1), small(lk1), small(lq2), small(lk2), small(sg),
        ],
        out_specs=_head_spec(out_shape, DIFF_V_DIM),
        out_shape=out_shape,
        scratch_shapes=_attn_scratch(DIFF_BLOCK, 2, DIFF_V_ROWS),
        compiler_params=pltpu.CompilerParams(
            dimension_semantics=("arbitrary",), vmem_limit_bytes=VMEM_LIMIT_BYTES),
        name="diff_attn",
    )(dqT, dk, dvT, lq1, lk1, lq2, lk2, sg)


def _out_kernel(yf_ref, yd_ref, gf_ref, gd_ref, x_ref, w_ref, g_ref, o_ref):
    tn = (((0,), (0,)), ((), ()))
    gated = lambda y_ref, gate_ref: (y_ref[0].astype(F32) * gate_ref[0].astype(F32)).astype(BF16)
    y = lax.dot_general(gated(yf_ref, gf_ref), w_ref[:WIDTH, :], tn, preferred_element_type=F32)
    y = y + lax.dot_general(gated(yd_ref, gd_ref), w_ref[WIDTH:, :], tn, preferred_element_type=F32)
    h = x_ref[...] + y
    o_ref[...] = h * lax.rsqrt(jnp.mean(h * h, axis=-1, keepdims=True) + EPS) * g_ref[...]


def _out_project(yT_fox, yT_diff, fgT, dgT, x2d, w_out, final_g):
    seq = x2d.shape[0]
    nblk, _, tm = yT_fox.shape
    chunk = pl.BlockSpec((1, WIDTH, tm), lambda i: (i, 0, 0))
    return pl.pallas_call(
        _out_kernel,
        grid=(nblk,),
        in_specs=[
            chunk, chunk, chunk, chunk,
            pl.BlockSpec((tm, D_MODEL), lambda i: (i, 0)),
            pl.BlockSpec(w_out.shape, lambda i: (0, 0)),
            pl.BlockSpec((1, D_MODEL), lambda i: (0, 0)),
        ],
        out_specs=pl.BlockSpec((tm, D_MODEL), lambda i: (i, 0)),
        out_shape=jax.ShapeDtypeStruct((seq, D_MODEL), F32),
        compiler_params=pltpu.CompilerParams(
            dimension_semantics=("arbitrary",), vmem_limit_bytes=VMEM_LIMIT_BYTES),
        name="out_proj",
    )(yT_fox, yT_diff, fgT, dgT, x2d, w_out, final_g)


_COL_SIZES = (("fq", WIDTH), ("fk", WIDTH), ("fv", WIDTH), ("fg", WIDTH), ("fz", N_HEADS_FOX),
              ("dq", WIDTH), ("dk", WIDTH), ("dv", WIDTH), ("dg", WIDTH))
_COL_OFF = {name: sum(size for _, size in _COL_SIZES[:j]) for j, (name, _) in enumerate(_COL_SIZES)}


def _prep_kernel(w_ref, place_ref, wt_ref, wfk_ref):
    n_rows, n_cols = w_ref.shape[1:]

    def group(name):
        off = _COL_OFF[name]
        lo = off // LANES * LANES
        hi = min(lo + WIDTH + (LANES if off > lo else 0), n_cols)
        return w_ref[0, :, lo:hi][:, off - lo:off - lo + WIDTH]

    for gi, name in enumerate(_T_GROUPS):
        wt_ref[gi * WIDTH:(gi + 1) * WIDTH, :] = group(name).T.astype(BF16)
    wfk_ref[...] = _dot(group("fk").astype(BF16), place_ref[...]).astype(BF16)
    off = _COL_OFF["fz"]
    lane = lax.broadcasted_iota(jnp.int32, (n_rows, LANES), 1)
    fz = jnp.where(lane < N_HEADS_FOX, w_ref[0, :, off:off + LANES], 0.0)
    wt_ref[_T_OFF["fz"]:_T_ROWS, :] = fz.T[:GATE_ROWS, :].astype(BF16)


def _prepare_weights(w):
    _, d, n_cols = w.shape
    place = np.zeros((WIDTH, N_HEADS_FOX * LANES), np.float32)
    for h in range(N_HEADS_FOX):
        place[h * HEAD_DIM + np.arange(HEAD_DIM), h * LANES + np.arange(HEAD_DIM)] = 1.0
    place = jnp.asarray(place, BF16)
    full = lambda shape: pl.BlockSpec(shape, lambda i: (0,) * len(shape))
    out_shape = (jax.ShapeDtypeStruct((_T_ROWS, d), BF16),
                 jax.ShapeDtypeStruct((d, N_HEADS_FOX * LANES), BF16))
    return pl.pallas_call(
        _prep_kernel,
        grid=(1,),
        in_specs=[pl.BlockSpec((1, d, n_cols), lambda i: (0, 0, 0), pipeline_mode=pl.Buffered(1)), full(place.shape)],
        out_specs=tuple(full(o.shape) for o in out_shape),
        out_shape=out_shape,
        compiler_params=pltpu.CompilerParams(
            dimension_semantics=("arbitrary",), vmem_limit_bytes=VMEM_LIMIT_BYTES),
        name="prep_weights",
    )(w, place)


def _bias_selector():
    sel = np.zeros((LANES, N_HEADS_FOX * LANES), np.float32)
    for p in range(N_BIAS_PIECES):
        for h in range(N_HEADS_FOX):
            sel[p * GATE_ROWS + h, h * LANES + HEAD_DIM + p] = 1.0
    return jnp.asarray(sel, BF16)


def kernel(x, norm_g, w_in, b_forget, lambda_q1, lambda_k1, lambda_q2, lambda_k2, subln_g, w_out, final_g):
    batch, seq, _ = x.shape
    assert batch == 1
    for blk in (FOX_BLOCK, DIFF_BLOCK):
        assert seq % blk == 0 and seq // blk >= 3 and blk % ROW_BLOCK == 0
    x2d = x[0]
    wt, wfk = _prepare_weights(w_in)
    bf = jnp.pad(b_forget[0].astype(F32), (0, GATE_ROWS - N_HEADS_FOX)).reshape(GATE_ROWS, 1)
    invf = (ROPE_THETA ** (-jnp.arange(0, HEAD_DIM, 2, dtype=F32) / HEAD_DIM)).reshape(HEAD_DIM // 2, 1)

    fqT, fk, fvT, fgT, dqT, dk, dvT, dgT = _project(
        x2d, norm_g[0].reshape(1, D_MODEL), wt, wfk, bf, invf, _bias_selector())

    yT_fox = _fox_attention(fqT, fk, fvT)
    row = lambda a: a[0].astype(F32).reshape(1, HEAD_DIM)
    yT_diff = _diff_attention(dqT, dk, dvT, row(lambda_q1), row(lambda_k1), row(lambda_q2), row(lambda_k2),
                              subln_g[0].astype(F32).reshape(DIFF_V_DIM, 1))
    out = _out_project(yT_fox, yT_diff, fgT, dgT, x2d, w_out[0].astype(BF16), final_g.reshape(1, D_MODEL))
    return out[None]
```

```python
import math

import jax
import jax.numpy as jnp
import numpy as np
from jax import lax
from jax.experimental import pallas as pl
from jax.experimental.pallas import tpu as pltpu

D_MODEL = 1024
HEAD_DIM = 64
N_HEADS_FOX = 8
N_HEADS_DIFF = 4
N_MAPS_DIFF = 2 * N_HEADS_DIFF
DIFF_V_DIM = 2 * HEAD_DIM
WIDTH = 512
CHUNK = 64
ROPE_THETA = 10000.0
EPS = 1e-6
LAMBDA_INIT = 0.8 - 0.6 * math.exp(-0.3 * 0)
SCALE = HEAD_DIM ** -0.5
LOG2E = math.log2(math.e)
Q_SCALE = SCALE * LOG2E

LANES = 128
ROW_BLOCK = 512
OUT_ROW_BLOCK = 1024
FOX_BLOCK = 2048
DIFF_BLOCK = 1024
SUB_KEYS = 256
FOX_Q_TILE = 256
DIFF_Q_TILE = 256
N_BIAS_PIECES = 3
BF16_SUBLANES = 16
GATE_ROWS = BF16_SUBLANES
FOX_V_ROWS = HEAD_DIM + BF16_SUBLANES
DIFF_V_ROWS = DIFF_V_DIM + BF16_SUBLANES
VMEM_LIMIT_BYTES = 56 * 1024 * 1024

F32 = jnp.float32
BF16 = jnp.bfloat16

_T_GROUPS = ("fq", "fv", "fg", "dq", "dk", "dv", "dg")
_T_OFF = {name: i * WIDTH for i, name in enumerate(_T_GROUPS + ("fz",))}
_T_ROWS = _T_OFF["fz"] + GATE_ROWS


def _dot_nt(a, b):
    return lax.dot_general(a, b, (((1,), (1,)), ((), ())), preferred_element_type=F32)


def _dot(a, b):
    return jnp.dot(a, b, preferred_element_type=F32)


def _split_bf16(x):
    hi = x.astype(BF16)
    r = x - hi.astype(F32)
    mid = r.astype(BF16)
    lo = (r - mid.astype(F32)).astype(BF16)
    return hi, mid, lo


def _proj_kernel(x_ref, ng_ref, wt_ref, wfk_ref, bf_ref, invf_ref, sel_ref,
                 fqT_ref, fk_ref, fvT_ref, fgT_ref, dqT_ref, dk_ref, dvT_ref, dgT_ref,
                 carry_ref):
    i = pl.program_id(0)
    tm = x_ref.shape[0]

    @pl.when(i == 0)
    def _():
        carry_ref[...] = jnp.zeros_like(carry_ref)

    x = x_ref[...]
    u = x * lax.rsqrt(jnp.mean(x * x, axis=-1, keepdims=True) + EPS) * ng_ref[...]
    ub = u.astype(BF16)

    def proj_t(name, width=WIDTH):
        off = _T_OFF[name]
        return _dot_nt(wt_ref[off:off + width, :], ub)


    zT = proj_t("fz", GATE_ROWS) + bf_ref[...]
    logf = jax.nn.log_sigmoid(zT)
    lane = lax.broadcasted_iota(jnp.int32, logf.shape, 1)
    shift = 1
    while shift < tm:
        logf = logf + jnp.where(lane >= shift, pltpu.roll(logf, shift, axis=1), 0.0)
        shift *= 2
    cum = logf + carry_ref[...]
    carry_ref[...] = cum[:, tm - 1:tm]
    pieces = [p.astype(F32) for p in _split_bf16(cum * (-LOG2E))]
    pieces.append(jnp.zeros((LANES - N_BIAS_PIECES * GATE_ROWS, tm), F32))
    pieces_t = jnp.concatenate(pieces, axis=0).T.astype(BF16)

    fg = proj_t("fg")
    fgT_ref[0] = (fg * jax.nn.sigmoid(fg)).astype(BF16)
    dg = proj_t("dg")
    dgT_ref[0] = (dg * jax.nn.sigmoid(dg)).astype(BF16)

    pos = (i * tm + lax.broadcasted_iota(jnp.int32, (1, tm), 1)).astype(F32)
    ang = pos * invf_ref[...]
    cos, sin = jnp.cos(ang), jnp.sin(ang)
    half = HEAD_DIM // 2

    def rope_t(xt):
        x1, x2 = xt[:half, :], xt[half:, :]
        return jnp.concatenate([x1 * cos - x2 * sin, x2 * cos + x1 * sin], axis=0)

    qT = proj_t("dq")
    for m in range(N_MAPS_DIFF):
        rows = slice(m * HEAD_DIM, (m + 1) * HEAD_DIM)
        dqT_ref[0, rows, :] = (rope_t(qT[rows, :]) * Q_SCALE).astype(BF16)
    kT = proj_t("dk")
    kT = jnp.concatenate([rope_t(kT[m * HEAD_DIM:(m + 1) * HEAD_DIM, :]) for m in range(N_MAPS_DIFF)], axis=0)
    dk_ref[...] = kT.T.astype(BF16)

    qT = proj_t("fq") * Q_SCALE
    row = lax.broadcasted_iota(jnp.int32, (LANES - HEAD_DIM, tm), 0)
    ones_rows = jnp.where(row < N_BIAS_PIECES, 1.0, 0.0).astype(BF16)
    for h in range(N_HEADS_FOX):
        fqT_ref[0, h * LANES:h * LANES + HEAD_DIM, :] = qT[h * HEAD_DIM:(h + 1) * HEAD_DIM, :].astype(BF16)
        fqT_ref[0, h * LANES + HEAD_DIM:(h + 1) * LANES, :] = ones_rows

    row = lax.broadcasted_iota(jnp.int32, (BF16_SUBLANES, tm), 0)
    denom_rows = jnp.where(row == 0, 1.0, 0.0).astype(BF16)
    vT = proj_t("fv").astype(BF16)
    for h in range(N_HEADS_FOX):
        fvT_ref[0, h * FOX_V_ROWS:h * FOX_V_ROWS + HEAD_DIM, :] = vT[h * HEAD_DIM:(h + 1) * HEAD_DIM, :]
        fvT_ref[0, h * FOX_V_ROWS + HEAD_DIM:(h + 1) * FOX_V_ROWS, :] = denom_rows
    vT = proj_t("dv").astype(BF16)
    for h in range(N_HEADS_DIFF):
        dvT_ref[0, h * DIFF_V_ROWS:h * DIFF_V_ROWS + DIFF_V_DIM, :] = vT[h * DIFF_V_DIM:(h + 1) * DIFF_V_DIM, :]
        dvT_ref[0, h * DIFF_V_ROWS + DIFF_V_DIM:(h + 1) * DIFF_V_ROWS, :] = denom_rows

    fk_ref[...] = (_dot(ub, wfk_ref[...]) + _dot(pieces_t, sel_ref[...])).astype(BF16)


def _project(x2d, norm_g, wt, wfk, bf, invf, sel):
    seq = x2d.shape[0]
    tm = ROW_BLOCK
    nblk = seq // tm
    const = lambda shape: pl.BlockSpec(shape, lambda i: (0,) * len(shape))
    chunk_t = lambda rows: pl.BlockSpec((1, rows, tm), lambda i: (i, 0, 0))
    chunked = lambda rows: jax.ShapeDtypeStruct((nblk, rows, tm), BF16)
    out_shape = (
        chunked(N_HEADS_FOX * LANES),
        jax.ShapeDtypeStruct((seq, N_HEADS_FOX * LANES), BF16),
        chunked(N_HEADS_FOX * FOX_V_ROWS),
        chunked(WIDTH),
        chunked(WIDTH),
        jax.ShapeDtypeStruct((seq, WIDTH), BF16),
        chunked(N_HEADS_DIFF * DIFF_V_ROWS),
        chunked(WIDTH),
    )
    return pl.pallas_call(
        _proj_kernel,
        grid=(nblk,),
        in_specs=[
            pl.BlockSpec((tm, D_MODEL), lambda i: (i, 0)),
            const((1, D_MODEL)),
            const(wt.shape), const(wfk.shape), const(bf.shape),
            const(invf.shape), const(sel.shape),
        ],
        out_specs=(
            chunk_t(N_HEADS_FOX * LANES),
            pl.BlockSpec((tm, N_HEADS_FOX * LANES), lambda i: (i, 0)),
            chunk_t(N_HEADS_FOX * FOX_V_ROWS), chunk_t(WIDTH),
            chunk_t(WIDTH),
            pl.BlockSpec((tm, WIDTH), lambda i: (i, 0)),
            chunk_t(N_HEADS_DIFF * DIFF_V_ROWS), chunk_t(WIDTH),
        ),
        out_shape=out_shape,
        scratch_shapes=[pltpu.VMEM((GATE_ROWS, 1), F32)],
        compiler_params=pltpu.CompilerParams(
            dimension_semantics=("arbitrary",), vmem_limit_bytes=VMEM_LIMIT_BYTES),
        name="proj",
    )(x2d, norm_g, wt, wfk, bf, invf, sel)


def _attn_head(load_rhs, finalize, mask_fn, k_ref, vT_ref, rhs_ref, s_ref, mblk_ref, m_ref, acc_ref,
               *, blk, n_maps, q_tile):
    nq = n_maps * blk
    v_rows, v_chunk = vT_ref.shape[1:]
    n_sub = blk // SUB_KEYS
    n_qblocks = k_ref.shape[0] // blk
    neg_inf = lambda cols: jnp.full((1, cols), -jnp.inf, F32)

    def tile_kind(diag, k0, q0):
        if not diag or k0 + SUB_KEYS <= q0:
            return "full"
        return "skip" if k0 >= q0 + q_tile else "mask"

    def step(consume, produce):
        if consume is not None:
            m_old = m_ref[...]
            m_new = jnp.maximum(m_old, mblk_ref[...])
            m_ref[...] = m_new
            alpha = jnp.exp2(m_old - m_new)
        for cq in range(nq // q_tile):
            cols = slice(cq * q_tile, (cq + 1) * q_tile)
            q0 = (cq * q_tile) % blk
            pv, m_next = None, None
            for j in range(n_sub):
                k0 = j * SUB_KEYS
                rows = slice(k0, k0 + SUB_KEYS)
                p_kind = tile_kind(produce[1], k0, q0) if produce is not None else "skip"
                c_kind = tile_kind(consume[1], k0, q0) if consume is not None else "skip"
                if p_kind != "skip":
                    row0 = produce[0] * blk + k0
                    if not isinstance(row0, int):
                        row0 = pl.multiple_of(row0, SUB_KEYS)
                    s = _dot(k_ref[pl.ds(row0, SUB_KEYS), :], rhs_ref[:, cols])
                    if p_kind == "mask":
                        key_pos = k0 + lax.broadcasted_iota(jnp.int32, s.shape, 0)
                        query_pos = q0 + lax.broadcasted_iota(jnp.int32, s.shape, 1)
                        s = jnp.where(mask_fn(key_pos, query_pos), s, -jnp.inf)
                if c_kind != "skip":
                    c, off = divmod(k0, v_chunk)
                    v_sub = vT_ref[consume[0] * (blk // v_chunk) + c, :, off:off + SUB_KEYS]
                    d = _dot(v_sub, jnp.exp2(s_ref[rows, cols] - m_new[:, cols]).astype(BF16))
                    pv = d if pv is None else pv + d
                if p_kind != "skip":
                    s_ref[rows, cols] = s
                    m_j = jnp.max(s, axis=0, keepdims=True)
                    m_next = m_j if m_next is None else jnp.maximum(m_next, m_j)
            if consume is not None:
                acc_ref[:, cols] = alpha[:, cols] * acc_ref[:, cols] + pv
            if produce is not None:
                mblk_ref[:, cols] = m_next

    def reset():
        m_ref[...] = neg_inf(nq)
        acc_ref[...] = jnp.zeros_like(acc_ref)

    def rest_of_query_block(q, produce_after):
        step((q, True), (0, False))

        def two_steps(i, _):
            kb = 1 + 2 * i
            step((kb - 1, False), (kb, False))
            step((kb, False), (kb + 1, False))

        n_pairs = (q - 1) // 2
        lax.fori_loop(0, n_pairs, two_steps, None)
        lax.fori_loop(1 + 2 * n_pairs, q, lambda kb, _: step((kb - 1, False), (kb, False)), None)
        if produce_after is not None:
            load_rhs(q + 1)
        step((q - 1, False), produce_after)
        finalize(q)

    reset()
    load_rhs(0)
    step(None, (0, True))
    load_rhs(1)
    step((0, True), (1, True))
    finalize(0)
    reset()

    def middle(q, _):
        rest_of_query_block(q, (q + 1, True))
        reset()

    lax.fori_loop(1, n_qblocks - 1, middle, None)
    rest_of_query_block(n_qblocks - 1, None)


def _fox_kernel(qT_ref, k_ref, vT_ref, o_ref, rhs_ref, s_ref, mblk_ref, m_ref, acc_ref):
    blk = s_ref.shape[0]
    chunk = qT_ref.shape[2]
    per_block = blk // chunk

    def load_rhs(q):
        for c in range(per_block):
            rhs_ref[:, c * chunk:(c + 1) * chunk] = qT_ref[q * per_block + c]

    def finalize(q):
        acc = acc_ref[...]
        o = acc[:HEAD_DIM, :] / acc[HEAD_DIM:HEAD_DIM + 1, :]
        for c in range(per_block):
            o_ref[q * per_block + c] = o[:, c * chunk:(c + 1) * chunk].astype(o_ref.dtype)

    def mask_fn(key_pos, query_pos):
        return key_pos <= query_pos

    _attn_head(load_rhs, finalize, mask_fn, k_ref, vT_ref, rhs_ref, s_ref, mblk_ref, m_ref, acc_ref,
               blk=blk, n_maps=1, q_tile=FOX_Q_TILE)


def _diff_kernel(qT_ref, k_ref, vT_ref, lq1_ref, lk1_ref, lq2_ref, lk2_ref, sg_ref, o_ref,
                 rhs_ref, s_ref, mblk_ref, m_ref, acc_ref):
    blk = s_ref.shape[0]
    chunk = qT_ref.shape[2]
    per_block = blk // chunk

    def load_rhs(q):
        zeros = jnp.zeros((HEAD_DIM, chunk), BF16)
        for c in range(per_block):
            q12 = qT_ref[q * per_block + c]
            rhs_ref[:, c * chunk:(c + 1) * chunk] = jnp.concatenate([q12[:HEAD_DIM], zeros], axis=0)
            rhs_ref[:, blk + c * chunk:blk + (c + 1) * chunk] = jnp.concatenate([zeros, q12[HEAD_DIM:]], axis=0)

    def finalize(q):
        acc = acc_ref[...]
        o = acc[:DIFF_V_DIM, :] / acc[DIFF_V_DIM:DIFF_V_DIM + 1, :]
        lam = (jnp.exp(jnp.sum(lq1_ref[...] * lk1_ref[...], axis=-1, keepdims=True))
               - jnp.exp(jnp.sum(lq2_ref[...] * lk2_ref[...], axis=-1, keepdims=True)) + LAMBDA_INIT)
        y = o[:, :blk] - lam * o[:, blk:]
        y = y * lax.rsqrt(jnp.mean(y * y, axis=0, keepdims=True) + EPS) * sg_ref[...]
        y = y * (1.0 - LAMBDA_INIT)
        for c in range(per_block):
            o_ref[q * per_block + c] = y[:, c * chunk:(c + 1) * chunk].astype(o_ref.dtype)

    def mask_fn(key_pos, query_pos):
        return key_pos // CHUNK <= query_pos // CHUNK

    _attn_head(load_rhs, finalize, mask_fn, k_ref, vT_ref, rhs_ref, s_ref, mblk_ref, m_ref, acc_ref,
               blk=blk, n_maps=2, q_tile=DIFF_Q_TILE)


def _head_spec(arr, rows):
    return pl.BlockSpec((arr.shape[0], rows, arr.shape[2]), lambda h: (0, h, 0))


def _attn_scratch(blk, n_maps, v_rows):
    nq = n_maps * blk
    return [pltpu.VMEM((LANES, nq), BF16), pltpu.VMEM((blk, nq), F32),
            pltpu.VMEM((1, nq), F32), pltpu.VMEM((1, nq), F32), pltpu.VMEM((v_rows, nq), F32)]


def _fox_attention(fqT, fk, fvT):
    seq = fk.shape[0]
    out_shape = jax.ShapeDtypeStruct((fqT.shape[0], WIDTH, fqT.shape[2]), BF16)
    return pl.pallas_call(
        _fox_kernel,
        grid=(N_HEADS_FOX,),
        in_specs=[
            _head_spec(fqT, LANES),
            pl.BlockSpec((seq, LANES), lambda h: (0, h)),
            _head_spec(fvT, FOX_V_ROWS),
        ],
        out_specs=_head_spec(out_shape, HEAD_DIM),
        out_shape=out_shape,
        scratch_shapes=_attn_scratch(FOX_BLOCK, 1, FOX_V_ROWS),
        compiler_params=pltpu.CompilerParams(
            dimension_semantics=("arbitrary",), vmem_limit_bytes=VMEM_LIMIT_BYTES),
        name="fox_attn",
    )(fqT, fk, fvT)


def _diff_attention(dqT, dk, dvT, lq1, lk1, lq2, lk2, sg):
    seq = dk.shape[0]
    small = lambda a: pl.BlockSpec(a.shape, lambda h: (0, 0))
    out_shape = jax.ShapeDtypeStruct(dqT.shape, BF16)
    return pl.pallas_call(
        _diff_kernel,
        grid=(N_HEADS_DIFF,),
        in_specs=[
            _head_spec(dqT, 2 * HEAD_DIM),
            pl.BlockSpec((seq, LANES), lambda h: (0, h)),
            _head_spec(dvT, DIFF_V_ROWS),
            small(lq1), small(lk1), small(lq2), small(lk2), small(sg),
        ],
        out_specs=_head_spec(out_shape, DIFF_V_DIM),
        out_shape=out_shape,
        scratch_shapes=_attn_scratch(DIFF_BLOCK, 2, DIFF_V_ROWS),
        compiler_params=pltpu.CompilerParams(
            dimension_semantics=("arbitrary",), vmem_limit_bytes=VMEM_LIMIT_BYTES),
        name="diff_attn",
    )(dqT, dk, dvT, lq1, lk1, lq2, lk2, sg)


def _out_kernel(yf_ref, yd_ref, gf_ref, gd_ref, x_ref, w_ref, g_ref, o_ref):
    tn = (((0,), (0,)), ((), ()))
    n_chunks, _, tm = yf_ref.shape
    for c in range(n_chunks):
        gated = lambda y_ref, gate_ref: (y_ref[c].astype(F32) * gate_ref[c].astype(F32)).astype(BF16)
        y = lax.dot_general(gated(yf_ref, gf_ref), w_ref[:WIDTH, :], tn, preferred_element_type=F32)
        y = y + lax.dot_general(gated(yd_ref, gd_ref), w_ref[WIDTH:, :], tn, preferred_element_type=F32)
        h = x_ref[c * tm:(c + 1) * tm, :] + y
        o_ref[c * tm:(c + 1) * tm, :] = h * lax.rsqrt(jnp.mean(h * h, axis=-1, keepdims=True) + EPS) * g_ref[...]


def _out_project(yT_fox, yT_diff, fgT, dgT, x2d, w_out, final_g):
    seq = x2d.shape[0]
    nblk, _, tm = yT_fox.shape
    per_step = OUT_ROW_BLOCK // tm
    chunk = pl.BlockSpec((per_step, WIDTH, tm), lambda i: (i, 0, 0))
    return pl.pallas_call(
        _out_kernel,
        grid=(nblk // per_step,),
        in_specs=[
            chunk, chunk, chunk, chunk,
            pl.BlockSpec((OUT_ROW_BLOCK, D_MODEL), lambda i: (i, 0)),
            pl.BlockSpec(w_out.shape, lambda i: (0, 0)),
            pl.BlockSpec((1, D_MODEL), lambda i: (0, 0)),
        ],
        out_specs=pl.BlockSpec((OUT_ROW_BLOCK, D_MODEL), lambda i: (i, 0)),
        out_shape=jax.ShapeDtypeStruct((seq, D_MODEL), F32),
        compiler_params=pltpu.CompilerParams(
            dimension_semantics=("arbitrary",), vmem_limit_bytes=VMEM_LIMIT_BYTES),
        name="out_proj",
    )(yT_fox, yT_diff, fgT, dgT, x2d, w_out, final_g)


_COL_SIZES = (("fq", WIDTH), ("fk", WIDTH), ("fv", WIDTH), ("fg", WIDTH), ("fz", N_HEADS_FOX),
              ("dq", WIDTH), ("dk", WIDTH), ("dv", WIDTH), ("dg", WIDTH))
_COL_OFF = {name: sum(size for _, size in _COL_SIZES[:j]) for j, (name, _) in enumerate(_COL_SIZES)}


def _prep_kernel(w_ref, place_ref, wt_ref, wfk_ref):
    d = w_ref.shape[2]

    def group(name, rows=WIDTH):
        off = _COL_OFF[name]
        return w_ref[0, off:off + rows, :]

    for gi, name in enumerate(_T_GROUPS):
        wt_ref[gi * WIDTH:(gi + 1) * WIDTH, :] = group(name).astype(BF16)
    fz = jnp.concatenate([group("fz", N_HEADS_FOX), jnp.zeros((GATE_ROWS - N_HEADS_FOX, d), F32)], axis=0)
    wt_ref[_T_OFF["fz"]:_T_ROWS, :] = fz.astype(BF16)
    wfk_ref[...] = _dot(group("fk").T.astype(BF16), place_ref[...]).astype(BF16)


def _prepare_weights(w):
    _, n_cols, d = w.shape
    place = np.zeros((WIDTH, N_HEADS_FOX * LANES), np.float32)
    for h in range(N_HEADS_FOX):
        place[h * HEAD_DIM + np.arange(HEAD_DIM), h * LANES + np.arange(HEAD_DIM)] = 1.0
    place = jnp.asarray(place, BF16)
    full = lambda shape: pl.BlockSpec(shape, lambda i: (0,) * len(shape))
    out_shape = (jax.ShapeDtypeStruct((_T_ROWS, d), BF16),
                 jax.ShapeDtypeStruct((d, N_HEADS_FOX * LANES), BF16))
    return pl.pallas_call(
        _prep_kernel,
        grid=(1,),
        in_specs=[pl.BlockSpec((1, n_cols, d), lambda i: (0, 0, 0), pipeline_mode=pl.Buffered(1)), full(place.shape)],
        out_specs=tuple(full(o.shape) for o in out_shape),
        out_shape=out_shape,
        compiler_params=pltpu.CompilerParams(
            dimension_semantics=("arbitrary",), vmem_limit_bytes=VMEM_LIMIT_BYTES),
        name="prep_weights",
    )(w, place)


def _bias_selector():
    sel = np.zeros((LANES, N_HEADS_FOX * LANES), np.float32)
    for p in range(N_BIAS_PIECES):
        for h in range(N_HEADS_FOX):
            sel[p * GATE_ROWS + h, h * LANES + HEAD_DIM + p] = 1.0
    return jnp.asarray(sel, BF16)


def kernel(x, norm_g, w_in, b_forget, lambda_q1, lambda_k1, lambda_q2, lambda_k2, subln_g, w_out, final_g):
    batch, seq, _ = x.shape
    assert batch == 1
    for blk in (FOX_BLOCK, DIFF_BLOCK):
        assert seq % blk == 0 and seq // blk >= 3 and blk % ROW_BLOCK == 0
    x2d = x[0]
    wt, wfk = _prepare_weights(jnp.swapaxes(w_in, 1, 2))
    bf = jnp.pad(b_forget[0].astype(F32), (0, GATE_ROWS - N_HEADS_FOX)).reshape(GATE_ROWS, 1)
    invf = (ROPE_THETA ** (-jnp.arange(0, HEAD_DIM, 2, dtype=F32) / HEAD_DIM)).reshape(HEAD_DIM // 2, 1)

    fqT, fk, fvT, fgT, dqT, dk, dvT, dgT = _project(
        x2d, norm_g[0].reshape(1, D_MODEL), wt, wfk, bf, invf, _bias_selector())

    yT_fox = _fox_attention(fqT, fk, fvT)
    row = lambda a: a[0].astype(F32).reshape(1, HEAD_DIM)
    yT_diff = _diff_attention(dqT, dk, dvT, row(lambda_q1), row(lambda_k1), row(lambda_q2), row(lambda_k2),
                              subln_g[0].astype(F32).reshape(DIFF_V_DIM, 1))
    out = _out_project(yT_fox, yT_diff, fgT, dgT, x2d, w_out[0].astype(BF16), final_g.reshape(1, D_MODEL))
    return out[None]
```

```python
import math

import jax
import jax.numpy as jnp
import numpy as np
from jax import lax
from jax.experimental import pallas as pl
from jax.experimental.pallas import tpu as pltpu

D_MODEL = 1024
HEAD_DIM = 64
N_HEADS_FOX = 8
N_HEADS_DIFF = 4
N_MAPS_DIFF = 2 * N_HEADS_DIFF
DIFF_V_DIM = 2 * HEAD_DIM
WIDTH = 512
CHUNK = 64
ROPE_THETA = 10000.0
EPS = 1e-6
LAMBDA_INIT = 0.8 - 0.6 * math.exp(-0.3 * 0)
SCALE = HEAD_DIM ** -0.5
LOG2E = math.log2(math.e)
Q_SCALE = SCALE * LOG2E

LANES = 128
ROW_BLOCK = 512
OUT_ROW_BLOCK = 1024
FOX_BLOCK = 2048
DIFF_BLOCK = 1024
SUB_KEYS = 256
FOX_Q_TILE = 256
DIFF_Q_TILE = 256
N_BIAS_PIECES = 3
BF16_SUBLANES = 16
GATE_ROWS = BF16_SUBLANES
FOX_V_ROWS = HEAD_DIM + BF16_SUBLANES
DIFF_V_ROWS = DIFF_V_DIM + BF16_SUBLANES
VMEM_LIMIT_BYTES = 56 * 1024 * 1024

F32 = jnp.float32
BF16 = jnp.bfloat16

_T_GROUPS = ("fq", "fv", "fg", "dq", "dk", "dv", "dg")
_T_OFF = {name: i * WIDTH for i, name in enumerate(_T_GROUPS + ("fz",))}
_T_ROWS = _T_OFF["fz"] + GATE_ROWS


def _dot_nt(a, b):
    return lax.dot_general(a, b, (((1,), (1,)), ((), ())), preferred_element_type=F32)


def _dot(a, b):
    return jnp.dot(a, b, preferred_element_type=F32)


def _split_bf16(x):
    hi = x.astype(BF16)
    r = x - hi.astype(F32)
    mid = r.astype(BF16)
    lo = (r - mid.astype(F32)).astype(BF16)
    return hi, mid, lo


def _proj_kernel(x_ref, ng_ref, wt_ref, wfk_ref, bf_ref, invf_ref, sel_ref,
                 fqT_ref, fk_ref, fvT_ref, fgT_ref, dqT_ref, dk_ref, dvT_ref, dgT_ref,
                 carry_ref):
    i = pl.program_id(0)
    tm = x_ref.shape[0]

    @pl.when(i == 0)
    def _():
        carry_ref[...] = jnp.zeros_like(carry_ref)

    x = x_ref[...]
    u = x * lax.rsqrt(jnp.mean(x * x, axis=-1, keepdims=True) + EPS) * ng_ref[...]
    ub = u.astype(BF16)

    def proj_t(name, width=WIDTH):
        off = _T_OFF[name]
        return _dot_nt(wt_ref[off:off + width, :], ub)


    zT = proj_t("fz", GATE_ROWS) + bf_ref[...]
    logf = jax.nn.log_sigmoid(zT)
    lane = lax.broadcasted_iota(jnp.int32, logf.shape, 1)
    shift = 1
    while shift < tm:
        logf = logf + jnp.where(lane >= shift, pltpu.roll(logf, shift, axis=1), 0.0)
        shift *= 2
    cum = logf + carry_ref[...]
    carry_ref[...] = cum[:, tm - 1:tm]
    pieces = [p.astype(F32) for p in _split_bf16(cum * (-LOG2E))]
    pieces.append(jnp.zeros((LANES - N_BIAS_PIECES * GATE_ROWS, tm), F32))
    pieces_t = jnp.concatenate(pieces, axis=0).T.astype(BF16)

    fg = proj_t("fg")
    fgT_ref[0] = (fg * jax.nn.sigmoid(fg)).astype(BF16)
    dg = proj_t("dg")
    dgT_ref[0] = (dg * jax.nn.sigmoid(dg)).astype(BF16)

    pos = (i * tm + lax.broadcasted_iota(jnp.int32, (1, tm), 1)).astype(F32)
    ang = pos * invf_ref[...]
    cos, sin = jnp.cos(ang), jnp.sin(ang)
    half = HEAD_DIM // 2

    def rope_t(xt):
        x1, x2 = xt[:half, :], xt[half:, :]
        return jnp.concatenate([x1 * cos - x2 * sin, x2 * cos + x1 * sin], axis=0)

    qT = proj_t("dq")
    for m in range(N_MAPS_DIFF):
        rows = slice(m * HEAD_DIM, (m + 1) * HEAD_DIM)
        dqT_ref[0, rows, :] = (rope_t(qT[rows, :]) * Q_SCALE).astype(BF16)
    kT = proj_t("dk")
    kT = jnp.concatenate([rope_t(kT[m * HEAD_DIM:(m + 1) * HEAD_DIM, :]) for m in range(N_MAPS_DIFF)], axis=0)
    dk_ref[...] = kT.T.astype(BF16)

    qT = proj_t("fq") * Q_SCALE
    row = lax.broadcasted_iota(jnp.int32, (LANES - HEAD_DIM, tm), 0)
    ones_rows = jnp.where(row < N_BIAS_PIECES, 1.0, 0.0).astype(BF16)
    for h in range(N_HEADS_FOX):
        fqT_ref[0, h * LANES:h * LANES + HEAD_DIM, :] = qT[h * HEAD_DIM:(h + 1) * HEAD_DIM, :].astype(BF16)
        fqT_ref[0, h * LANES + HEAD_DIM:(h + 1) * LANES, :] = ones_rows

    row = lax.broadcasted_iota(jnp.int32, (BF16_SUBLANES, tm), 0)
    denom_rows = jnp.where(row == 0, 1.0, 0.0).astype(BF16)
    vT = proj_t("fv").astype(BF16)
    for h in range(N_HEADS_FOX):
        fvT_ref[0, h * FOX_V_ROWS:h * FOX_V_ROWS + HEAD_DIM, :] = vT[h * HEAD_DIM:(h + 1) * HEAD_DIM, :]
        fvT_ref[0, h * FOX_V_ROWS + HEAD_DIM:(h + 1) * FOX_V_ROWS, :] = denom_rows
    vT = proj_t("dv").astype(BF16)
    for h in range(N_HEADS_DIFF):
        dvT_ref[0, h * DIFF_V_ROWS:h * DIFF_V_ROWS + DIFF_V_DIM, :] = vT[h * DIFF_V_DIM:(h + 1) * DIFF_V_DIM, :]
        dvT_ref[0, h * DIFF_V_ROWS + DIFF_V_DIM:(h + 1) * DIFF_V_ROWS, :] = denom_rows

    fk_ref[...] = (_dot(ub, wfk_ref[...]) + _dot(pieces_t, sel_ref[...])).astype(BF16)


def _project(x2d, norm_g, wt, wfk, bf, invf, sel):
    seq = x2d.shape[0]
    tm = ROW_BLOCK
    nblk = seq // tm
    const = lambda shape: pl.BlockSpec(shape, lambda i: (0,) * len(shape))
    chunk_t = lambda rows: pl.BlockSpec((1, rows, tm), lambda i: (i, 0, 0))
    chunked = lambda rows: jax.ShapeDtypeStruct((nblk, rows, tm), BF16)
    out_shape = (
        chunked(N_HEADS_FOX * LANES),
        jax.ShapeDtypeStruct((seq, N_HEADS_FOX * LANES), BF16),
        chunked(N_HEADS_FOX * FOX_V_ROWS),
        chunked(WIDTH),
        chunked(WIDTH),
        jax.ShapeDtypeStruct((seq, WIDTH), BF16),
        chunked(N_HEADS_DIFF * DIFF_V_ROWS),
        chunked(WIDTH),
    )
    return pl.pallas_call(
        _proj_kernel,
        grid=(nblk,),
        in_specs=[
            pl.BlockSpec((tm, D_MODEL), lambda i: (i, 0)),
            const((1, D_MODEL)),
            const(wt.shape), const(wfk.shape), const(bf.shape),
            const(invf.shape), const(sel.shape),
        ],
        out_specs=(
            chunk_t(N_HEADS_FOX * LANES),
            pl.BlockSpec((tm, N_HEADS_FOX * LANES), lambda i: (i, 0)),
            chunk_t(N_HEADS_FOX * FOX_V_ROWS), chunk_t(WIDTH),
            chunk_t(WIDTH),
            pl.BlockSpec((tm, WIDTH), lambda i: (i, 0)),
            chunk_t(N_HEADS_DIFF * DIFF_V_ROWS), chunk_t(WIDTH),
        ),
        out_shape=out_shape,
        scratch_shapes=[pltpu.VMEM((GATE_ROWS, 1), F32)],
        compiler_params=pltpu.CompilerParams(
            dimension_semantics=("arbitrary",), vmem_limit_bytes=VMEM_LIMIT_BYTES),
        name="proj",
    )(x2d, norm_g, wt, wfk, bf, invf, sel)


def _attn_head(load_rhs, finalize, mask_fn, k_ref, vT_ref, rhs_ref, s_ref, mblk_ref, m_ref, acc_ref,
               *, blk, n_maps, q_tile, tiles_abreast):
    nq = n_maps * blk
    v_rows, v_chunk = vT_ref.shape[1:]
    n_sub = blk // SUB_KEYS
    n_qblocks = k_ref.shape[0] // blk
    neg_inf = lambda cols: jnp.full((1, cols), -jnp.inf, F32)

    def tile_kind(diag, k0, q0):
        if not diag or k0 + SUB_KEYS <= q0:
            return "full"
        return "skip" if k0 >= q0 + q_tile else "mask"

    def step(consume, produce):
        if consume is not None:
            m_old = m_ref[...]
            m_new = jnp.maximum(m_old, mblk_ref[...])
            m_ref[...] = m_new
            alpha = jnp.exp2(m_old - m_new)
        n_cq = nq // q_tile
        for cq0 in range(0, n_cq, tiles_abreast):
            group = range(cq0, min(cq0 + tiles_abreast, n_cq))
            cols = {cq: slice(cq * q_tile, (cq + 1) * q_tile) for cq in group}
            q0 = {cq: (cq * q_tile) % blk for cq in group}
            pv = {cq: None for cq in group}
            m_next = {cq: None for cq in group}
            for j in range(n_sub):
                k0 = j * SUB_KEYS
                rows = slice(k0, k0 + SUB_KEYS)
                p_kind = {cq: tile_kind(produce[1], k0, q0[cq]) if produce is not None else "skip" for cq in group}
                c_kind = {cq: tile_kind(consume[1], k0, q0[cq]) if consume is not None else "skip" for cq in group}
                s = {}
                for cq in group:
                    if p_kind[cq] == "skip":
                        continue
                    row0 = produce[0] * blk + k0
                    if not isinstance(row0, int):
                        row0 = pl.multiple_of(row0, SUB_KEYS)
                    s[cq] = _dot(k_ref[pl.ds(row0, SUB_KEYS), :], rhs_ref[:, cols[cq]])
                    if p_kind[cq] == "mask":
                        key_pos = k0 + lax.broadcasted_iota(jnp.int32, s[cq].shape, 0)
                        query_pos = q0[cq] + lax.broadcasted_iota(jnp.int32, s[cq].shape, 1)
                        s[cq] = jnp.where(mask_fn(key_pos, query_pos), s[cq], -jnp.inf)
                for cq in group:
                    if c_kind[cq] == "skip":
                        continue
                    c, off = divmod(k0, v_chunk)
                    v_sub = vT_ref[consume[0] * (blk // v_chunk) + c, :, off:off + SUB_KEYS]
                    d = _dot(v_sub, jnp.exp2(s_ref[rows, cols[cq]] - m_new[:, cols[cq]]).astype(BF16))
                    pv[cq] = d if pv[cq] is None else pv[cq] + d
                for cq in s:
                    s_ref[rows, cols[cq]] = s[cq]
                    m_j = jnp.max(s[cq], axis=0, keepdims=True)
                    m_next[cq] = m_j if m_next[cq] is None else jnp.maximum(m_next[cq], m_j)
            for cq in group:
                if consume is not None:
                    acc_ref[:, cols[cq]] = alpha[:, cols[cq]] * acc_ref[:, cols[cq]] + pv[cq]
                if produce is not None:
                    mblk_ref[:, cols[cq]] = m_next[cq]

    def reset():
        m_ref[...] = neg_inf(nq)
        acc_ref[...] = jnp.zeros_like(acc_ref)

    def key_blocks_1_to(q):
        def two_steps(i, _):
            kb = 1 + 2 * i
            step((kb - 1, False), (kb, False))
            step((kb, False), (kb + 1, False))

        n_pairs = (q - 1) // 2
        lax.fori_loop(0, n_pairs, two_steps, None)
        lax.fori_loop(1 + 2 * n_pairs, q, lambda kb, _: step((kb - 1, False), (kb, False)), None)

    def next_query_block(q, consume):
        load_rhs(q)
        step(consume, (q, True))
        finalize(q - 1)
        reset()
        step((q, True), (0, False))

    reset()
    load_rhs(0)
    step(None, (0, True))
    next_query_block(1, (0, True))

    def middle(q, _):
        key_blocks_1_to(q)
        next_query_block(q + 1, (q - 1, False))

    lax.fori_loop(1, n_qblocks - 1, middle, None)
    last = n_qblocks - 1
    key_blocks_1_to(last)
    step((last - 1, False), None)
    finalize(last)


def _fox_kernel(qT_ref, k_ref, vT_ref, o_ref, rhs_ref, s_ref, mblk_ref, m_ref, acc_ref):
    blk = s_ref.shape[0]
    chunk = qT_ref.shape[2]
    per_block = blk // chunk

    def load_rhs(q):
        for c in range(per_block):
            rhs_ref[:, c * chunk:(c + 1) * chunk] = qT_ref[q * per_block + c]

    def finalize(q):
        acc = acc_ref[...]
        o = acc[:HEAD_DIM, :] / acc[HEAD_DIM:HEAD_DIM + 1, :]
        for c in range(per_block):
            o_ref[q * per_block + c] = o[:, c * chunk:(c + 1) * chunk].astype(o_ref.dtype)

    def mask_fn(key_pos, query_pos):
        return key_pos <= query_pos

    _attn_head(load_rhs, finalize, mask_fn, k_ref, vT_ref, rhs_ref, s_ref, mblk_ref, m_ref, acc_ref,
               blk=blk, n_maps=1, q_tile=FOX_Q_TILE, tiles_abreast=2)


def _diff_kernel(qT_ref, k_ref, vT_ref, lq1_ref, lk1_ref, lq2_ref, lk2_ref, sg_ref, o_ref,
                 rhs_ref, s_ref, mblk_ref, m_ref, acc_ref):
    blk = s_ref.shape[0]
    chunk = qT_ref.shape[2]
    per_block = blk // chunk

    def load_rhs(q):
        zeros = jnp.zeros((HEAD_DIM, chunk), BF16)
        for c in range(per_block):
            q12 = qT_ref[q * per_block + c]
            rhs_ref[:, c * chunk:(c + 1) * chunk] = jnp.concatenate([q12[:HEAD_DIM], zeros], axis=0)
            rhs_ref[:, blk + c * chunk:blk + (c + 1) * chunk] = jnp.concatenate([zeros, q12[HEAD_DIM:]], axis=0)

    def finalize(q):
        acc = acc_ref[...]
        o = acc[:DIFF_V_DIM, :] / acc[DIFF_V_DIM:DIFF_V_DIM + 1, :]
        lam = (jnp.exp(jnp.sum(lq1_ref[...] * lk1_ref[...], axis=-1, keepdims=True))
               - jnp.exp(jnp.sum(lq2_ref[...] * lk2_ref[...], axis=-1, keepdims=True)) + LAMBDA_INIT)
        y = o[:, :blk] - lam * o[:, blk:]
        y = y * lax.rsqrt(jnp.mean(y * y, axis=0, keepdims=True) + EPS) * sg_ref[...]
        y = y * (1.0 - LAMBDA_INIT)
        for c in range(per_block):
            o_ref[q * per_block + c] = y[:, c * chunk:(c + 1) * chunk].astype(o_ref.dtype)

    def mask_fn(key_pos, query_pos):
        return key_pos // CHUNK <= query_pos // CHUNK

    _attn_head(load_rhs, finalize, mask_fn, k_ref, vT_ref, rhs_ref, s_ref, mblk_ref, m_ref, acc_ref,
               blk=blk, n_maps=2, q_tile=DIFF_Q_TILE, tiles_abreast=1)


def _head_spec(arr, rows):
    return pl.BlockSpec((arr.shape[0], rows, arr.shape[2]), lambda h: (0, h, 0))


def _attn_scratch(blk, n_maps, v_rows):
    nq = n_maps * blk
    return [pltpu.VMEM((LANES, nq), BF16), pltpu.VMEM((blk, nq), F32),
            pltpu.VMEM((1, nq), F32), pltpu.VMEM((1, nq), F32), pltpu.VMEM((v_rows, nq), F32)]


def _fox_attention(fqT, fk, fvT):
    seq = fk.shape[0]
    out_shape = jax.ShapeDtypeStruct((fqT.shape[0], WIDTH, fqT.shape[2]), BF16)
    return pl.pallas_call(
        _fox_kernel,
        grid=(N_HEADS_FOX,),
        in_specs=[
            _head_spec(fqT, LANES),
            pl.BlockSpec((seq, LANES), lambda h: (0, h)),
            _head_spec(fvT, FOX_V_ROWS),
        ],
        out_specs=_head_spec(out_shape, HEAD_DIM),
        out_shape=out_shape,
        scratch_shapes=_attn_scratch(FOX_BLOCK, 1, FOX_V_ROWS),
        compiler_params=pltpu.CompilerParams(
            dimension_semantics=("arbitrary",), vmem_limit_bytes=VMEM_LIMIT_BYTES),
        name="fox_attn",
    )(fqT, fk, fvT)


def _diff_attention(dqT, dk, dvT, lq1, lk1, lq2, lk2, sg):
    seq = dk.shape[0]
    small = lambda a: pl.BlockSpec(a.shape, lambda h: (0, 0))
    out_shape = jax.ShapeDtypeStruct(dqT.shape, BF16)
    return pl.pallas_call(
        _diff_kernel,
        grid=(N_HEADS_DIFF,),
        in_specs=[
            _head_spec(dqT, 2 * HEAD_DIM),
            pl.BlockSpec((seq, LANES), lambda h: (0, h)),
            _head_spec(dvT, DIFF_V_ROWS),
            small(lq1), small(lk1), small(lq2), small(lk2), small(sg),
        ],
        out_specs=_head_spec(out_shape, DIFF_V_DIM),
        out_shape=out_shape,
        scratch_shapes=_attn_scratch(DIFF_BLOCK, 2, DIFF_V_ROWS),
        compiler_params=pltpu.CompilerParams(
            dimension_semantics=("arbitrary",), vmem_limit_bytes=VMEM_LIMIT_BYTES),
        name="diff_attn",
    )(dqT, dk, dvT, lq1, lk1, lq2, lk2, sg)


def _out_kernel(yf_ref, yd_ref, gf_ref, gd_ref, x_ref, w_ref, g_ref, o_ref):
    tn = (((0,), (0,)), ((), ()))
    n_chunks, _, tm = yf_ref.shape
    for c in range(n_chunks):
        gated = lambda y_ref, gate_ref: (y_ref[c].astype(F32) * gate_ref[c].astype(F32)).astype(BF16)
        y = lax.dot_general(gated(yf_ref, gf_ref), w_ref[:WIDTH, :], tn, preferred_element_type=F32)
        y = y + lax.dot_general(gated(yd_ref, gd_ref), w_ref[WIDTH:, :], tn, preferred_element_type=F32)
        h = x_ref[c * tm:(c + 1) * tm, :] + y
        o_ref[c * tm:(c + 1) * tm, :] = h * lax.rsqrt(jnp.mean(h * h, axis=-1, keepdims=True) + EPS) * g_ref[...]


def _out_project(yT_fox, yT_diff, fgT, dgT, x2d, w_out, final_g):
    seq = x2d.shape[0]
    nblk, _, tm = yT_fox.shape
    per_step = OUT_ROW_BLOCK // tm
    chunk = pl.BlockSpec((per_step, WIDTH, tm), lambda i: (i, 0, 0))
    return pl.pallas_call(
        _out_kernel,
        grid=(nblk // per_step,),
        in_specs=[
            chunk, chunk, chunk, chunk,
            pl.BlockSpec((OUT_ROW_BLOCK, D_MODEL), lambda i: (i, 0)),
            pl.BlockSpec(w_out.shape, lambda i: (0, 0)),
            pl.BlockSpec((1, D_MODEL), lambda i: (0, 0)),
        ],
        out_specs=pl.BlockSpec((OUT_ROW_BLOCK, D_MODEL), lambda i: (i, 0)),
        out_shape=jax.ShapeDtypeStruct((seq, D_MODEL), F32),
        compiler_params=pltpu.CompilerParams(
            dimension_semantics=("arbitrary",), vmem_limit_bytes=VMEM_LIMIT_BYTES),
        name="out_proj",
    )(yT_fox, yT_diff, fgT, dgT, x2d, w_out, final_g)


_COL_SIZES = (("fq", WIDTH), ("fk", WIDTH), ("fv", WIDTH), ("fg", WIDTH), ("fz", N_HEADS_FOX),
              ("dq", WIDTH), ("dk", WIDTH), ("dv", WIDTH), ("dg", WIDTH))
_COL_OFF = {name: sum(size for _, size in _COL_SIZES[:j]) for j, (name, _) in enumerate(_COL_SIZES)}


def _prep_kernel(w_ref, place_ref, wt_ref, wfk_ref):
    d = w_ref.shape[2]

    def group(name, rows=WIDTH):
        off = _COL_OFF[name]
        return w_ref[0, off:off + rows, :]

    for gi, name in enumerate(_T_GROUPS):
        wt_ref[gi * WIDTH:(gi + 1) * WIDTH, :] = group(name).astype(BF16)
    fz = jnp.concatenate([group("fz", N_HEADS_FOX), jnp.zeros((GATE_ROWS - N_HEADS_FOX, d), F32)], axis=0)
    wt_ref[_T_OFF["fz"]:_T_ROWS, :] = fz.astype(BF16)
    wfk_ref[...] = _dot(group("fk").T.astype(BF16), place_ref[...]).astype(BF16)


def _prepare_weights(w):
    _, n_cols, d = w.shape
    place = np.zeros((WIDTH, N_HEADS_FOX * LANES), np.float32)
    for h in range(N_HEADS_FOX):
        place[h * HEAD_DIM + np.arange(HEAD_DIM), h * LANES + np.arange(HEAD_DIM)] = 1.0
    place = jnp.asarray(place, BF16)
    full = lambda shape: pl.BlockSpec(shape, lambda i: (0,) * len(shape))
    out_shape = (jax.ShapeDtypeStruct((_T_ROWS, d), BF16),
                 jax.ShapeDtypeStruct((d, N_HEADS_FOX * LANES), BF16))
    return pl.pallas_call(
        _prep_kernel,
        grid=(1,),
        in_specs=[pl.BlockSpec((1, n_cols, d), lambda i: (0, 0, 0), pipeline_mode=pl.Buffered(1)), full(place.shape)],
        out_specs=tuple(full(o.shape) for o in out_shape),
        out_shape=out_shape,
        compiler_params=pltpu.CompilerParams(
            dimension_semantics=("arbitrary",), vmem_limit_bytes=VMEM_LIMIT_BYTES),
        name="prep_weights",
    )(w, place)


def _bias_selector():
    sel = np.zeros((LANES, N_HEADS_FOX * LANES), np.float32)
    for p in range(N_BIAS_PIECES):
        for h in range(N_HEADS_FOX):
            sel[p * GATE_ROWS + h, h * LANES + HEAD_DIM + p] = 1.0
    return jnp.asarray(sel, BF16)


def kernel(x, norm_g, w_in, b_forget, lambda_q1, lambda_k1, lambda_q2, lambda_k2, subln_g, w_out, final_g):
    batch, seq, _ = x.shape
    assert batch == 1
    for blk in (FOX_BLOCK, DIFF_BLOCK):
        assert seq % blk == 0 and seq // blk >= 3 and blk % ROW_BLOCK == 0
    x2d = x[0]
    wt, wfk = _prepare_weights(jnp.swapaxes(w_in, 1, 2))
    bf = jnp.pad(b_forget[0].astype(F32), (0, GATE_ROWS - N_HEADS_FOX)).reshape(GATE_ROWS, 1)
    invf = (ROPE_THETA ** (-jnp.arange(0, HEAD_DIM, 2, dtype=F32) / HEAD_DIM)).reshape(HEAD_DIM // 2, 1)

    fqT, fk, fvT, fgT, dqT, dk, dvT, dgT = _project(
        x2d, norm_g[0].reshape(1, D_MODEL), wt, wfk, bf, invf, _bias_selector())

    yT_fox = _fox_attention(fqT, fk, fvT)
    row = lambda a: a[0].astype(F32).reshape(1, HEAD_DIM)
    yT_diff = _diff_attention(dqT, dk, dvT, row(lambda_q1), row(lambda_k1), row(lambda_q2), row(lambda_k2),
                              subln_g[0].astype(F32).reshape(DIFF_V_DIM, 1))
    out = _out_project(yT_fox, yT_diff, fgT, dgT, x2d, w_out[0].astype(BF16), final_g.reshape(1, D_MODEL))
    return out[None]
```

```python
import math

import jax
import jax.numpy as jnp
import numpy as np
from jax import lax
from jax.experimental import pallas as pl
from jax.experimental.pallas import tpu as pltpu

D_MODEL = 1024
HEAD_DIM = 64
N_HEADS_FOX = 8
N_HEADS_DIFF = 4
N_MAPS_DIFF = 2 * N_HEADS_DIFF
DIFF_V_DIM = 2 * HEAD_DIM
WIDTH = 512
CHUNK = 64
ROPE_THETA = 10000.0
EPS = 1e-6
LAMBDA_INIT = 0.8 - 0.6 * math.exp(-0.3 * 0)
SCALE = HEAD_DIM ** -0.5
LOG2E = math.log2(math.e)
Q_SCALE = SCALE * LOG2E

LANES = 128
ROW_BLOCK = 512
OUT_ROW_BLOCK = 1024
FOX_BLOCK = 2048
DIFF_BLOCK = 1024
SUB_KEYS = 256
FOX_Q_TILE = 256
DIFF_Q_TILE = 256
N_BIAS_PIECES = 3
BF16_SUBLANES = 16
GATE_ROWS = BF16_SUBLANES
FOX_V_ROWS = HEAD_DIM + BF16_SUBLANES
DIFF_V_ROWS = DIFF_V_DIM + BF16_SUBLANES
VMEM_LIMIT_BYTES = 56 * 1024 * 1024

F32 = jnp.float32
BF16 = jnp.bfloat16

_T_GROUPS = ("fq", "fv", "fg", "dq", "dk", "dv", "dg")
_T_OFF = {name: i * WIDTH for i, name in enumerate(_T_GROUPS + ("fz",))}
_T_ROWS = _T_OFF["fz"] + GATE_ROWS


def _dot_nt(a, b):
    return lax.dot_general(a, b, (((1,), (1,)), ((), ())), preferred_element_type=F32)


def _dot(a, b):
    return jnp.dot(a, b, preferred_element_type=F32)


def _split_bf16(x):
    hi = x.astype(BF16)
    r = x - hi.astype(F32)
    mid = r.astype(BF16)
    lo = (r - mid.astype(F32)).astype(BF16)
    return hi, mid, lo


def _proj_kernel(x_ref, ng_ref, wt_ref, wfk_ref, bf_ref, invf_ref, sel_ref,
                 fqT_ref, fk_ref, fvT_ref, fgT_ref, dqT_ref, dk_ref, dvT_ref, dgT_ref,
                 carry_ref):
    i = pl.program_id(0)
    tm = x_ref.shape[0]

    @pl.when(i == 0)
    def _():
        carry_ref[...] = jnp.zeros_like(carry_ref)

    x = x_ref[...]
    u = x * lax.rsqrt(jnp.mean(x * x, axis=-1, keepdims=True) + EPS) * ng_ref[...]
    ub = u.astype(BF16)

    def proj_t(name, width=WIDTH):
        off = _T_OFF[name]
        return _dot_nt(wt_ref[off:off + width, :], ub)


    zT = proj_t("fz", GATE_ROWS) + bf_ref[...]
    logf = jax.nn.log_sigmoid(zT)
    lane = lax.broadcasted_iota(jnp.int32, logf.shape, 1)
    shift = 1
    while shift < tm:
        logf = logf + jnp.where(lane >= shift, pltpu.roll(logf, shift, axis=1), 0.0)
        shift *= 2
    cum = logf + carry_ref[...]
    carry_ref[...] = cum[:, tm - 1:tm]
    pieces = [p.astype(F32) for p in _split_bf16(cum * (-LOG2E))]
    pieces.append(jnp.zeros((LANES - N_BIAS_PIECES * GATE_ROWS, tm), F32))
    pieces_t = jnp.concatenate(pieces, axis=0).T.astype(BF16)

    fg = proj_t("fg")
    fgT_ref[0] = (fg * jax.nn.sigmoid(fg)).astype(BF16)
    dg = proj_t("dg")
    dgT_ref[0] = (dg * jax.nn.sigmoid(dg)).astype(BF16)

    pos = (i * tm + lax.broadcasted_iota(jnp.int32, (1, tm), 1)).astype(F32)
    ang = pos * invf_ref[...]
    cos, sin = jnp.cos(ang), jnp.sin(ang)
    half = HEAD_DIM // 2

    def rope_t(xt):
        x1, x2 = xt[:half, :], xt[half:, :]
        return jnp.concatenate([x1 * cos - x2 * sin, x2 * cos + x1 * sin], axis=0)

    qT = proj_t("dq")
    for m in range(N_MAPS_DIFF):
        rows = slice(m * HEAD_DIM, (m + 1) * HEAD_DIM)
        dqT_ref[0, rows, :] = (rope_t(qT[rows, :]) * Q_SCALE).astype(BF16)
    kT = proj_t("dk")
    kT = jnp.concatenate([rope_t(kT[m * HEAD_DIM:(m + 1) * HEAD_DIM, :]) for m in range(N_MAPS_DIFF)], axis=0)
    dk_ref[...] = kT.T.astype(BF16)

    qT = proj_t("fq") * Q_SCALE
    row = lax.broadcasted_iota(jnp.int32, (LANES - HEAD_DIM, tm), 0)
    ones_rows = jnp.where(row < N_BIAS_PIECES, 1.0, 0.0).astype(BF16)
    for h in range(N_HEADS_FOX):
        fqT_ref[0, h * LANES:h * LANES + HEAD_DIM, :] = qT[h * HEAD_DIM:(h + 1) * HEAD_DIM, :].astype(BF16)
        fqT_ref[0, h * LANES + HEAD_DIM:(h + 1) * LANES, :] = ones_rows

    row = lax.broadcasted_iota(jnp.int32, (BF16_SUBLANES, tm), 0)
    denom_rows = jnp.where(row == 0, 1.0, 0.0).astype(BF16)
    vT = proj_t("fv").astype(BF16)
    for h in range(N_HEADS_FOX):
        fvT_ref[0, h * FOX_V_ROWS:h * FOX_V_ROWS + HEAD_DIM, :] = vT[h * HEAD_DIM:(h + 1) * HEAD_DIM, :]
        fvT_ref[0, h * FOX_V_ROWS + HEAD_DIM:(h + 1) * FOX_V_ROWS, :] = denom_rows
    vT = proj_t("dv").astype(BF16)
    for h in range(N_HEADS_DIFF):
        dvT_ref[0, h * DIFF_V_ROWS:h * DIFF_V_ROWS + DIFF_V_DIM, :] = vT[h * DIFF_V_DIM:(h + 1) * DIFF_V_DIM, :]
        dvT_ref[0, h * DIFF_V_ROWS + DIFF_V_DIM:(h + 1) * DIFF_V_ROWS, :] = denom_rows

    fk_ref[...] = (_dot(ub, wfk_ref[...]) + _dot(pieces_t, sel_ref[...])).astype(BF16)


def _project(x2d, norm_g, wt, wfk, bf, invf, sel):
    seq = x2d.shape[0]
    tm = ROW_BLOCK
    nblk = seq // tm
    const = lambda shape: pl.BlockSpec(shape, lambda i: (0,) * len(shape))
    chunk_t = lambda rows: pl.BlockSpec((1, rows, tm), lambda i: (i, 0, 0))
    chunked = lambda rows: jax.ShapeDtypeStruct((nblk, rows, tm), BF16)
    out_shape = (
        chunked(N_HEADS_FOX * LANES),
        jax.ShapeDtypeStruct((seq, N_HEADS_FOX * LANES), BF16),
        chunked(N_HEADS_FOX * FOX_V_ROWS),
        chunked(WIDTH),
        chunked(WIDTH),
        jax.ShapeDtypeStruct((seq, WIDTH), BF16),
        chunked(N_HEADS_DIFF * DIFF_V_ROWS),
        chunked(WIDTH),
    )
    return pl.pallas_call(
        _proj_kernel,
        grid=(nblk,),
        in_specs=[
            pl.BlockSpec((tm, D_MODEL), lambda i: (i, 0)),
            const((1, D_MODEL)),
            const(wt.shape), const(wfk.shape), const(bf.shape),
            const(invf.shape), const(sel.shape),
        ],
        out_specs=(
            chunk_t(N_HEADS_FOX * LANES),
            pl.BlockSpec((tm, N_HEADS_FOX * LANES), lambda i: (i, 0)),
            chunk_t(N_HEADS_FOX * FOX_V_ROWS), chunk_t(WIDTH),
            chunk_t(WIDTH),
            pl.BlockSpec((tm, WIDTH), lambda i: (i, 0)),
            chunk_t(N_HEADS_DIFF * DIFF_V_ROWS), chunk_t(WIDTH),
        ),
        out_shape=out_shape,
        scratch_shapes=[pltpu.VMEM((GATE_ROWS, 1), F32)],
        compiler_params=pltpu.CompilerParams(
            dimension_semantics=("arbitrary",), vmem_limit_bytes=VMEM_LIMIT_BYTES),
        name="proj",
    )(x2d, norm_g, wt, wfk, bf, invf, sel)


def _attn_head(load_rhs, finalize, mask_fn, k_ref, vT_ref, rhs_ref, s_ref, mblk_ref, m_ref, acc_ref,
               *, blk, n_maps, q_tile, tiles_abreast, steps_per_trip):
    nq = n_maps * blk
    v_rows, v_chunk = vT_ref.shape[1:]
    n_sub = blk // SUB_KEYS
    n_qblocks = k_ref.shape[0] // blk
    neg_inf = lambda cols: jnp.full((1, cols), -jnp.inf, F32)

    def tile_kind(diag, k0, q0):
        if not diag or k0 + SUB_KEYS <= q0:
            return "full"
        return "skip" if k0 >= q0 + q_tile else "mask"

    def step(consume, produce):
        if consume is not None:
            m_old = m_ref[...]
            m_new = jnp.maximum(m_old, mblk_ref[...])
            m_ref[...] = m_new
            alpha = jnp.exp2(m_old - m_new)
        n_cq = nq // q_tile
        for cq0 in range(0, n_cq, tiles_abreast):
            group = range(cq0, min(cq0 + tiles_abreast, n_cq))
            cols = {cq: slice(cq * q_tile, (cq + 1) * q_tile) for cq in group}
            q0 = {cq: (cq * q_tile) % blk for cq in group}
            pv = {cq: None for cq in group}
            m_next = {cq: None for cq in group}
            for j in range(n_sub):
                k0 = j * SUB_KEYS
                rows = slice(k0, k0 + SUB_KEYS)
                p_kind = {cq: tile_kind(produce[1], k0, q0[cq]) if produce is not None else "skip" for cq in group}
                c_kind = {cq: tile_kind(consume[1], k0, q0[cq]) if consume is not None else "skip" for cq in group}
                s = {}
                for cq in group:
                    if p_kind[cq] == "skip":
                        continue
                    row0 = produce[0] * blk + k0
                    if not isinstance(row0, int):
                        row0 = pl.multiple_of(row0, SUB_KEYS)
                    s[cq] = _dot(k_ref[pl.ds(row0, SUB_KEYS), :], rhs_ref[:, cols[cq]])
                    if p_kind[cq] == "mask":
                        key_pos = k0 + lax.broadcasted_iota(jnp.int32, s[cq].shape, 0)
                        query_pos = q0[cq] + lax.broadcasted_iota(jnp.int32, s[cq].shape, 1)
                        s[cq] = jnp.where(mask_fn(key_pos, query_pos), s[cq], -jnp.inf)
                for cq in group:
                    if c_kind[cq] == "skip":
                        continue
                    c, off = divmod(k0, v_chunk)
                    v_sub = vT_ref[consume[0] * (blk // v_chunk) + c, :, off:off + SUB_KEYS]
                    d = _dot(v_sub, jnp.exp2(s_ref[rows, cols[cq]] - m_new[:, cols[cq]]).astype(BF16))
                    pv[cq] = d if pv[cq] is None else pv[cq] + d
                for cq in s:
                    s_ref[rows, cols[cq]] = s[cq]
                    m_j = jnp.max(s[cq], axis=0, keepdims=True)
                    m_next[cq] = m_j if m_next[cq] is None else jnp.maximum(m_next[cq], m_j)
            for cq in group:
                if consume is not None:
                    acc_ref[:, cols[cq]] = alpha[:, cols[cq]] * acc_ref[:, cols[cq]] + pv[cq]
                if produce is not None:
                    mblk_ref[:, cols[cq]] = m_next[cq]

    def reset():
        m_ref[...] = neg_inf(nq)
        acc_ref[...] = jnp.zeros_like(acc_ref)

    def key_blocks_1_to(q):
        def trip(i, _):
            for t in range(steps_per_trip):
                kb = 1 + steps_per_trip * i + t
                step((kb - 1, False), (kb, False))

        n_trips = (q - 1) // steps_per_trip
        lax.fori_loop(0, n_trips, trip, None)
        lax.fori_loop(1 + steps_per_trip * n_trips, q, lambda kb, _: step((kb - 1, False), (kb, False)), None)

    def next_query_block(q, consume):
        load_rhs(q)
        step(consume, (q, True))
        finalize(q - 1)
        reset()
        step((q, True), (0, False))

    reset()
    load_rhs(0)
    step(None, (0, True))
    next_query_block(1, (0, True))

    def middle(q, _):
        key_blocks_1_to(q)
        next_query_block(q + 1, (q - 1, False))

    lax.fori_loop(1, n_qblocks - 1, middle, None)
    last = n_qblocks - 1
    key_blocks_1_to(last)
    step((last - 1, False), None)
    finalize(last)


def _fox_kernel(qT_ref, k_ref, vT_ref, o_ref, rhs_ref, s_ref, mblk_ref, m_ref, acc_ref):
    blk = s_ref.shape[0]
    chunk = qT_ref.shape[2]
    per_block = blk // chunk

    def load_rhs(q):
        for c in range(per_block):
            rhs_ref[:, c * chunk:(c + 1) * chunk] = qT_ref[q * per_block + c]

    def finalize(q):
        acc = acc_ref[...]
        o = acc[:HEAD_DIM, :] / acc[HEAD_DIM:HEAD_DIM + 1, :]
        for c in range(per_block):
            o_ref[q * per_block + c] = o[:, c * chunk:(c + 1) * chunk].astype(o_ref.dtype)

    def mask_fn(key_pos, query_pos):
        return key_pos <= query_pos

    _attn_head(load_rhs, finalize, mask_fn, k_ref, vT_ref, rhs_ref, s_ref, mblk_ref, m_ref, acc_ref,
               blk=blk, n_maps=1, q_tile=FOX_Q_TILE, tiles_abreast=2, steps_per_trip=2)


def _diff_kernel(qT_ref, k_ref, vT_ref, lq1_ref, lk1_ref, lq2_ref, lk2_ref, sg_ref, o_ref,
                 rhs_ref, s_ref, mblk_ref, m_ref, acc_ref):
    blk = s_ref.shape[0]
    chunk = qT_ref.shape[2]
    per_block = blk // chunk

    def load_rhs(q):
        zeros = jnp.zeros((HEAD_DIM, chunk), BF16)
        for c in range(per_block):
            q12 = qT_ref[q * per_block + c]
            rhs_ref[:, c * chunk:(c + 1) * chunk] = jnp.concatenate([q12[:HEAD_DIM], zeros], axis=0)
            rhs_ref[:, blk + c * chunk:blk + (c + 1) * chunk] = jnp.concatenate([zeros, q12[HEAD_DIM:]], axis=0)

    def finalize(q):
        acc = acc_ref[...]
        o = acc[:DIFF_V_DIM, :] / acc[DIFF_V_DIM:DIFF_V_DIM + 1, :]
        lam = (jnp.exp(jnp.sum(lq1_ref[...] * lk1_ref[...], axis=-1, keepdims=True))
               - jnp.exp(jnp.sum(lq2_ref[...] * lk2_ref[...], axis=-1, keepdims=True)) + LAMBDA_INIT)
        y = o[:, :blk] - lam * o[:, blk:]
        y = y * lax.rsqrt(jnp.mean(y * y, axis=0, keepdims=True) + EPS) * sg_ref[...]
        y = y * (1.0 - LAMBDA_INIT)
        for c in range(per_block):
            o_ref[q * per_block + c] = y[:, c * chunk:(c + 1) * chunk].astype(o_ref.dtype)

    def mask_fn(key_pos, query_pos):
        return key_pos // CHUNK <= query_pos // CHUNK

    _attn_head(load_rhs, finalize, mask_fn, k_ref, vT_ref, rhs_ref, s_ref, mblk_ref, m_ref, acc_ref,
               blk=blk, n_maps=2, q_tile=DIFF_Q_TILE, tiles_abreast=1, steps_per_trip=4)


def _head_spec(arr, rows):
    return pl.BlockSpec((arr.shape[0], rows, arr.shape[2]), lambda h: (0, h, 0))


def _attn_scratch(blk, n_maps, v_rows):
    nq = n_maps * blk
    return [pltpu.VMEM((LANES, nq), BF16), pltpu.VMEM((blk, nq), F32),
            pltpu.VMEM((1, nq), F32), pltpu.VMEM((1, nq), F32), pltpu.VMEM((v_rows, nq), F32)]


def _fox_attention(fqT, fk, fvT):
    seq = fk.shape[0]
    out_shape = jax.ShapeDtypeStruct((fqT.shape[0], WIDTH, fqT.shape[2]), BF16)
    return pl.pallas_call(
        _fox_kernel,
        grid=(N_HEADS_FOX,),
        in_specs=[
            _head_spec(fqT, LANES),
            pl.BlockSpec((seq, LANES), lambda h: (0, h)),
            _head_spec(fvT, FOX_V_ROWS),
        ],
        out_specs=_head_spec(out_shape, HEAD_DIM),
        out_shape=out_shape,
        scratch_shapes=_attn_scratch(FOX_BLOCK, 1, FOX_V_ROWS),
        compiler_params=pltpu.CompilerParams(
            dimension_semantics=("arbitrary",), vmem_limit_bytes=VMEM_LIMIT_BYTES),
        name="fox_attn",
    )(fqT, fk, fvT)


def _diff_attention(dqT, dk, dvT, lq1, lk1, lq2, lk2, sg):
    seq = dk.shape[0]
    small = lambda a: pl.BlockSpec(a.shape, lambda h: (0, 0))
    out_shape = jax.ShapeDtypeStruct(dqT.shape, BF16)
    return pl.pallas_call(
        _diff_kernel,
        grid=(N_HEADS_DIFF,),
        in_specs=[
            _head_spec(dqT, 2 * HEAD_DIM),
            pl.BlockSpec((seq, LANES), lambda h: (0, h)),
            _head_spec(dvT, DIFF_V_ROWS),
            small(lq1), small(lk1), small(lq2), small(lk2), small(sg),
        ],
        out_specs=_head_spec(out_shape, DIFF_V_DIM),
        out_shape=out_shape,
        scratch_shapes=_attn_scratch(DIFF_BLOCK, 2, DIFF_V_ROWS),
        compiler_params=pltpu.CompilerParams(
            dimension_semantics=("arbitrary",), vmem_limit_bytes=VMEM_LIMIT_BYTES),
        name="diff_attn",
    )(dqT, dk, dvT, lq1, lk1, lq2, lk2, sg)


def _out_kernel(yf_ref, yd_ref, gf_ref, gd_ref, x_ref, w_ref, g_ref, o_ref):
    tn = (((0,), (0,)), ((), ()))
    n_chunks, _, tm = yf_ref.shape
    for c in range(n_chunks):
        gated = lambda y_ref, gate_ref: (y_ref[c].astype(F32) * gate_ref[c].astype(F32)).astype(BF16)
        y = lax.dot_general(gated(yf_ref, gf_ref), w_ref[:WIDTH, :], tn, preferred_element_type=F32)
        y = y + lax.dot_general(gated(yd_ref, gd_ref), w_ref[WIDTH:, :], tn, preferred_element_type=F32)
        h = x_ref[c * tm:(c + 1) * tm, :] + y
        o_ref[c * tm:(c + 1) * tm, :] = h * lax.rsqrt(jnp.mean(h * h, axis=-1, keepdims=True) + EPS) * g_ref[...]


def _out_project(yT_fox, yT_diff, fgT, dgT, x2d, w_out, final_g):
    seq = x2d.shape[0]
    nblk, _, tm = yT_fox.shape
    per_step = OUT_ROW_BLOCK // tm
    chunk = pl.BlockSpec((per_step, WIDTH, tm), lambda i: (i, 0, 0))
    return pl.pallas_call(
        _out_kernel,
        grid=(nblk // per_step,),
        in_specs=[
            chunk, chunk, chunk, chunk,
            pl.BlockSpec((OUT_ROW_BLOCK, D_MODEL), lambda i: (i, 0)),
            pl.BlockSpec(w_out.shape, lambda i: (0, 0)),
            pl.BlockSpec((1, D_MODEL), lambda i: (0, 0)),
        ],
        out_specs=pl.BlockSpec((OUT_ROW_BLOCK, D_MODEL), lambda i: (i, 0)),
        out_shape=jax.ShapeDtypeStruct((seq, D_MODEL), F32),
        compiler_params=pltpu.CompilerParams(
            dimension_semantics=("arbitrary",), vmem_limit_bytes=VMEM_LIMIT_BYTES),
        name="out_proj",
    )(yT_fox, yT_diff, fgT, dgT, x2d, w_out, final_g)


_COL_SIZES = (("fq", WIDTH), ("fk", WIDTH), ("fv", WIDTH), ("fg", WIDTH), ("fz", N_HEADS_FOX),
              ("dq", WIDTH), ("dk", WIDTH), ("dv", WIDTH), ("dg", WIDTH))
_COL_OFF = {name: sum(size for _, size in _COL_SIZES[:j]) for j, (name, _) in enumerate(_COL_SIZES)}


def _prep_kernel(w_ref, place_ref, wt_ref, wfk_ref):
    d = w_ref.shape[2]

    def group(name, rows=WIDTH):
        off = _COL_OFF[name]
        return w_ref[0, off:off + rows, :]

    for gi, name in enumerate(_T_GROUPS):
        wt_ref[gi * WIDTH:(gi + 1) * WIDTH, :] = group(name).astype(BF16)
    fz = jnp.concatenate([group("fz", N_HEADS_FOX), jnp.zeros((GATE_ROWS - N_HEADS_FOX, d), F32)], axis=0)
    wt_ref[_T_OFF["fz"]:_T_ROWS, :] = fz.astype(BF16)
    wfk_ref[...] = _dot(group("fk").T.astype(BF16), place_ref[...]).astype(BF16)


def _prepare_weights(w):
    _, n_cols, d = w.shape
    place = np.zeros((WIDTH, N_HEADS_FOX * LANES), np.float32)
    for h in range(N_HEADS_FOX):
        place[h * HEAD_DIM + np.arange(HEAD_DIM), h * LANES + np.arange(HEAD_DIM)] = 1.0
    place = jnp.asarray(place, BF16)
    full = lambda shape: pl.BlockSpec(shape, lambda i: (0,) * len(shape))
    out_shape = (jax.ShapeDtypeStruct((_T_ROWS, d), BF16),
                 jax.ShapeDtypeStruct((d, N_HEADS_FOX * LANES), BF16))
    return pl.pallas_call(
        _prep_kernel,
        grid=(1,),
        in_specs=[pl.BlockSpec((1, n_cols, d), lambda i: (0, 0, 0), pipeline_mode=pl.Buffered(1)), full(place.shape)],
        out_specs=tuple(full(o.shape) for o in out_shape),
        out_shape=out_shape,
        compiler_params=pltpu.CompilerParams(
            dimension_semantics=("arbitrary",), vmem_limit_bytes=VMEM_LIMIT_BYTES),
        name="prep_weights",
    )(w, place)


def _bias_selector():
    sel = np.zeros((LANES, N_HEADS_FOX * LANES), np.float32)
    for p in range(N_BIAS_PIECES):
        for h in range(N_HEADS_FOX):
            sel[p * GATE_ROWS + h, h * LANES + HEAD_DIM + p] = 1.0
    return jnp.asarray(sel, BF16)


def kernel(x, norm_g, w_in, b_forget, lambda_q1, lambda_k1, lambda_q2, lambda_k2, subln_g, w_out, final_g):
    batch, seq, _ = x.shape
    assert batch == 1
    for blk in (FOX_BLOCK, DIFF_BLOCK):
        assert seq % blk == 0 and seq // blk >= 3 and blk % ROW_BLOCK == 0
    x2d = x[0]
    wt, wfk = _prepare_weights(jnp.swapaxes(w_in, 1, 2))
    bf = jnp.pad(b_forget[0].astype(F32), (0, GATE_ROWS - N_HEADS_FOX)).reshape(GATE_ROWS, 1)
    invf = (ROPE_THETA ** (-jnp.arange(0, HEAD_DIM, 2, dtype=F32) / HEAD_DIM)).reshape(HEAD_DIM // 2, 1)

    fqT, fk, fvT, fgT, dqT, dk, dvT, dgT = _project(
        x2d, norm_g[0].reshape(1, D_MODEL), wt, wfk, bf, invf, _bias_selector())

    yT_fox = _fox_attention(fqT, fk, fvT)
    row = lambda a: a[0].astype(F32).reshape(1, HEAD_DIM)
    yT_diff = _diff_attention(dqT, dk, dvT, row(lambda_q1), row(lambda_k1), row(lambda_q2), row(lambda_k2),
                              subln_g[0].astype(F32).reshape(DIFF_V_DIM, 1))
    out = _out_project(yT_fox, yT_diff, fgT, dgT, x2d, w_out[0].astype(BF16), final_g.reshape(1, D_MODEL))
    return out[None]
```

```python
import math

import jax
import jax.numpy as jnp
import numpy as np
from jax import lax
from jax.experimental import pallas as pl
from jax.experimental.pallas import tpu as pltpu

D_MODEL = 1024
HEAD_DIM = 64
N_HEADS_FOX = 8
N_HEADS_DIFF = 4
N_MAPS_DIFF = 2 * N_HEADS_DIFF
DIFF_V_DIM = 2 * HEAD_DIM
WIDTH = 512
CHUNK = 64
ROPE_THETA = 10000.0
EPS = 1e-6
LAMBDA_INIT = 0.8 - 0.6 * math.exp(-0.3 * 0)
SCALE = HEAD_DIM ** -0.5
LOG2E = math.log2(math.e)
Q_SCALE = SCALE * LOG2E

LANES = 128
ROW_BLOCK = 512
OUT_ROW_BLOCK = 1024
FOX_BLOCK = 2048
DIFF_BLOCK = 1024
SUB_KEYS = 256
FOX_Q_TILE = 256
DIFF_Q_TILE = 256
N_BIAS_PIECES = 3
BF16_SUBLANES = 16
GATE_ROWS = BF16_SUBLANES
FOX_V_ROWS = HEAD_DIM + BF16_SUBLANES
DIFF_V_ROWS = DIFF_V_DIM + BF16_SUBLANES
VMEM_LIMIT_BYTES = 56 * 1024 * 1024

F32 = jnp.float32
BF16 = jnp.bfloat16

_T_GROUPS = ("fq", "fv", "fg", "dq", "dk", "dv", "dg")
_T_OFF = {name: i * WIDTH for i, name in enumerate(_T_GROUPS + ("fz",))}
_T_ROWS = _T_OFF["fz"] + GATE_ROWS


def _dot_nt(a, b):
    return lax.dot_general(a, b, (((1,), (1,)), ((), ())), preferred_element_type=F32)


def _dot(a, b):
    return jnp.dot(a, b, preferred_element_type=F32)


def _split_bf16(x):
    hi = x.astype(BF16)
    r = x - hi.astype(F32)
    mid = r.astype(BF16)
    lo = (r - mid.astype(F32)).astype(BF16)
    return hi, mid, lo


def _proj_kernel(x_ref, ng_ref, wt_ref, wfk_ref, bf_ref, invf_ref, sel_ref,
                 fqT_ref, fk_ref, fvT_ref, fgT_ref, dqT_ref, dk_ref, dvT_ref, dgT_ref,
                 carry_ref):
    i = pl.program_id(0)
    tm = x_ref.shape[0]

    @pl.when(i == 0)
    def _():
        carry_ref[...] = jnp.zeros_like(carry_ref)

    x = x_ref[...]
    u = x * lax.rsqrt(jnp.mean(x * x, axis=-1, keepdims=True) + EPS) * ng_ref[...]
    ub = u.astype(BF16)

    def proj_t(name, width=WIDTH):
        off = _T_OFF[name]
        return _dot_nt(wt_ref[off:off + width, :], ub)


    zT = proj_t("fz", GATE_ROWS) + bf_ref[...]
    logf = jax.nn.log_sigmoid(zT)
    lane = lax.broadcasted_iota(jnp.int32, logf.shape, 1)
    shift = 1
    while shift < tm:
        logf = logf + jnp.where(lane >= shift, pltpu.roll(logf, shift, axis=1), 0.0)
        shift *= 2
    cum = logf + carry_ref[...]
    carry_ref[...] = cum[:, tm - 1:tm]
    pieces = [p.astype(F32) for p in _split_bf16(cum * (-LOG2E))]
    pieces.append(jnp.zeros((LANES - N_BIAS_PIECES * GATE_ROWS, tm), F32))
    pieces_t = jnp.concatenate(pieces, axis=0).T.astype(BF16)

    fg = proj_t("fg")
    fgT_ref[0] = (fg * jax.nn.sigmoid(fg)).astype(BF16)
    dg = proj_t("dg")
    dgT_ref[0] = (dg * jax.nn.sigmoid(dg)).astype(BF16)

    pos = (i * tm + lax.broadcasted_iota(jnp.int32, (1, tm), 1)).astype(F32)
    ang = pos * invf_ref[...]
    cos, sin = jnp.cos(ang), jnp.sin(ang)
    half = HEAD_DIM // 2

    def rope_t(xt):
        x1, x2 = xt[:half, :], xt[half:, :]
        return jnp.concatenate([x1 * cos - x2 * sin, x2 * cos + x1 * sin], axis=0)

    qT = proj_t("dq")
    for m in range(N_MAPS_DIFF):
        rows = slice(m * HEAD_DIM, (m + 1) * HEAD_DIM)
        dqT_ref[0, rows, :] = (rope_t(qT[rows, :]) * Q_SCALE).astype(BF16)
    kT = proj_t("dk")
    kT = jnp.concatenate([rope_t(kT[m * HEAD_DIM:(m + 1) * HEAD_DIM, :]) for m in range(N_MAPS_DIFF)], axis=0)
    dk_ref[...] = kT.T.astype(BF16)

    qT = proj_t("fq") * Q_SCALE
    row = lax.broadcasted_iota(jnp.int32, (LANES - HEAD_DIM, tm), 0)
    ones_rows = jnp.where(row < N_BIAS_PIECES, 1.0, 0.0).astype(BF16)
    for h in range(N_HEADS_FOX):
        fqT_ref[0, h * LANES:h * LANES + HEAD_DIM, :] = qT[h * HEAD_DIM:(h + 1) * HEAD_DIM, :].astype(BF16)
        fqT_ref[0, h * LANES + HEAD_DIM:(h + 1) * LANES, :] = ones_rows

    row = lax.broadcasted_iota(jnp.int32, (BF16_SUBLANES, tm), 0)
    denom_rows = jnp.where(row == 0, 1.0, 0.0).astype(BF16)
    vT = proj_t("fv").astype(BF16)
    for h in range(N_HEADS_FOX):
        fvT_ref[0, h * FOX_V_ROWS:h * FOX_V_ROWS + HEAD_DIM, :] = vT[h * HEAD_DIM:(h + 1) * HEAD_DIM, :]
        fvT_ref[0, h * FOX_V_ROWS + HEAD_DIM:(h + 1) * FOX_V_ROWS, :] = denom_rows
    vT = proj_t("dv").astype(BF16)
    for h in range(N_HEADS_DIFF):
        dvT_ref[0, h * DIFF_V_ROWS:h * DIFF_V_ROWS + DIFF_V_DIM, :] = vT[h * DIFF_V_DIM:(h + 1) * DIFF_V_DIM, :]
        dvT_ref[0, h * DIFF_V_ROWS + DIFF_V_DIM:(h + 1) * DIFF_V_ROWS, :] = denom_rows

    fk_ref[...] = (_dot(ub, wfk_ref[...]) + _dot(pieces_t, sel_ref[...])).astype(BF16)


def _project(x2d, norm_g, wt, wfk, bf, invf, sel):
    seq = x2d.shape[0]
    tm = ROW_BLOCK
    nblk = seq // tm
    const = lambda shape: pl.BlockSpec(shape, lambda i: (0,) * len(shape))
    chunk_t = lambda rows: pl.BlockSpec((1, rows, tm), lambda i: (i, 0, 0))
    chunked = lambda rows: jax.ShapeDtypeStruct((nblk, rows, tm), BF16)
    out_shape = (
        chunked(N_HEADS_FOX * LANES),
        jax.ShapeDtypeStruct((seq, N_HEADS_FOX * LANES), BF16),
        chunked(N_HEADS_FOX * FOX_V_ROWS),
        chunked(WIDTH),
        chunked(WIDTH),
        jax.ShapeDtypeStruct((seq, WIDTH), BF16),
        chunked(N_HEADS_DIFF * DIFF_V_ROWS),
        chunked(WIDTH),
    )
    return pl.pallas_call(
        _proj_kernel,
        grid=(nblk,),
        in_specs=[
            pl.BlockSpec((tm, D_MODEL), lambda i: (i, 0)),
            const((1, D_MODEL)),
            const(wt.shape), const(wfk.shape), const(bf.shape),
            const(invf.shape), const(sel.shape),
        ],
        out_specs=(
            chunk_t(N_HEADS_FOX * LANES),
            pl.BlockSpec((tm, N_HEADS_FOX * LANES), lambda i: (i, 0)),
            chunk_t(N_HEADS_FOX * FOX_V_ROWS), chunk_t(WIDTH),
            chunk_t(WIDTH),
            pl.BlockSpec((tm, WIDTH), lambda i: (i, 0)),
            chunk_t(N_HEADS_DIFF * DIFF_V_ROWS), chunk_t(WIDTH),
        ),
        out_shape=out_shape,
        scratch_shapes=[pltpu.VMEM((GATE_ROWS, 1), F32)],
        compiler_params=pltpu.CompilerParams(
            dimension_semantics=("arbitrary",), vmem_limit_bytes=VMEM_LIMIT_BYTES),
        name="proj",
    )(x2d, norm_g, wt, wfk, bf, invf, sel)


def _attn_head(load_rhs, finalize, mask_fn, k_ref, vT_ref, rhs_ref, s_ref, mblk_ref, m_ref, acc_ref,
               *, blk, n_maps, q_tile, tiles_abreast, trip_sizes):
    nq = n_maps * blk
    v_rows, v_chunk = vT_ref.shape[1:]
    n_sub = blk // SUB_KEYS
    n_qblocks = k_ref.shape[0] // blk
    neg_inf = lambda cols: jnp.full((1, cols), -jnp.inf, F32)

    def tile_kind(diag, k0, q0):
        if not diag or k0 + SUB_KEYS <= q0:
            return "full"
        return "skip" if k0 >= q0 + q_tile else "mask"

    def step(consume, produce):
        if consume is not None:
            m_old = m_ref[...]
            m_new = jnp.maximum(m_old, mblk_ref[...])
            m_ref[...] = m_new
            alpha = jnp.exp2(m_old - m_new)
        n_cq = nq // q_tile
        for cq0 in range(0, n_cq, tiles_abreast):
            group = range(cq0, min(cq0 + tiles_abreast, n_cq))
            cols = {cq: slice(cq * q_tile, (cq + 1) * q_tile) for cq in group}
            q0 = {cq: (cq * q_tile) % blk for cq in group}
            pv = {cq: None for cq in group}
            m_next = {cq: None for cq in group}
            for j in range(n_sub):
                k0 = j * SUB_KEYS
                rows = slice(k0, k0 + SUB_KEYS)
                p_kind = {cq: tile_kind(produce[1], k0, q0[cq]) if produce is not None else "skip" for cq in group}
                c_kind = {cq: tile_kind(consume[1], k0, q0[cq]) if consume is not None else "skip" for cq in group}
                s = {}
                for cq in group:
                    if p_kind[cq] == "skip":
                        continue
                    row0 = produce[0] * blk + k0
                    if not isinstance(row0, int):
                        row0 = pl.multiple_of(row0, SUB_KEYS)
                    s[cq] = _dot(k_ref[pl.ds(row0, SUB_KEYS), :], rhs_ref[:, cols[cq]])
                    if p_kind[cq] == "mask":
                        key_pos = k0 + lax.broadcasted_iota(jnp.int32, s[cq].shape, 0)
                        query_pos = q0[cq] + lax.broadcasted_iota(jnp.int32, s[cq].shape, 1)
                        s[cq] = jnp.where(mask_fn(key_pos, query_pos), s[cq], -jnp.inf)
                for cq in group:
                    if c_kind[cq] == "skip":
                        continue
                    c, off = divmod(k0, v_chunk)
                    v_sub = vT_ref[consume[0] * (blk // v_chunk) + c, :, off:off + SUB_KEYS]
                    d = _dot(v_sub, jnp.exp2(s_ref[rows, cols[cq]] - m_new[:, cols[cq]]).astype(BF16))
                    pv[cq] = d if pv[cq] is None else pv[cq] + d
                for cq in s:
                    s_ref[rows, cols[cq]] = s[cq]
                    m_j = jnp.max(s[cq], axis=0, keepdims=True)
                    m_next[cq] = m_j if m_next[cq] is None else jnp.maximum(m_next[cq], m_j)
            for cq in group:
                if consume is not None:
                    acc_ref[:, cols[cq]] = alpha[:, cols[cq]] * acc_ref[:, cols[cq]] + pv[cq]
                if produce is not None:
                    mblk_ref[:, cols[cq]] = m_next[cq]

    def reset():
        m_ref[...] = neg_inf(nq)
        acc_ref[...] = jnp.zeros_like(acc_ref)

    def key_blocks_1_to(q):
        first = 1
        for n_steps in trip_sizes:
            def trip(i, _, first=first, n_steps=n_steps):
                for t in range(n_steps):
                    kb = first + n_steps * i + t
                    step((kb - 1, False), (kb, False))

            n_trips = (q - first) // n_steps
            lax.fori_loop(0, n_trips, trip, None)
            first = first + n_steps * n_trips
        lax.fori_loop(first, q, lambda kb, _: step((kb - 1, False), (kb, False)), None)

    def next_query_block(q, consume):
        load_rhs(q)
        step(consume, (q, True))
        finalize(q - 1)
        reset()
        step((q, True), (0, False))

    reset()
    load_rhs(0)
    step(None, (0, True))
    next_query_block(1, (0, True))

    def middle(q, _):
        key_blocks_1_to(q)
        next_query_block(q + 1, (q - 1, False))

    lax.fori_loop(1, n_qblocks - 1, middle, None)
    last = n_qblocks - 1
    key_blocks_1_to(last)
    step((last - 1, False), None)
    finalize(last)


def _fox_kernel(qT_ref, k_ref, vT_ref, o_ref, rhs_ref, s_ref, mblk_ref, m_ref, acc_ref):
    blk = s_ref.shape[0]
    chunk = qT_ref.shape[2]
    per_block = blk // chunk

    def load_rhs(q):
        for c in range(per_block):
            rhs_ref[:, c * chunk:(c + 1) * chunk] = qT_ref[q * per_block + c]

    def finalize(q):
        acc = acc_ref[...]
        o = acc[:HEAD_DIM, :] / acc[HEAD_DIM:HEAD_DIM + 1, :]
        for c in range(per_block):
            o_ref[q * per_block + c] = o[:, c * chunk:(c + 1) * chunk].astype(o_ref.dtype)

    def mask_fn(key_pos, query_pos):
        return key_pos <= query_pos

    _attn_head(load_rhs, finalize, mask_fn, k_ref, vT_ref, rhs_ref, s_ref, mblk_ref, m_ref, acc_ref,
               blk=blk, n_maps=1, q_tile=FOX_Q_TILE, tiles_abreast=2, trip_sizes=(2,))


def _diff_kernel(qT_ref, k_ref, vT_ref, lq1_ref, lk1_ref, lq2_ref, lk2_ref, sg_ref, o_ref,
                 rhs_ref, s_ref, mblk_ref, m_ref, acc_ref):
    blk = s_ref.shape[0]
    chunk = qT_ref.shape[2]
    per_block = blk // chunk

    def load_rhs(q):
        zeros = jnp.zeros((HEAD_DIM, chunk), BF16)
        for c in range(per_block):
            q12 = qT_ref[q * per_block + c]
            rhs_ref[:, c * chunk:(c + 1) * chunk] = jnp.concatenate([q12[:HEAD_DIM], zeros], axis=0)
            rhs_ref[:, blk + c * chunk:blk + (c + 1) * chunk] = jnp.concatenate([zeros, q12[HEAD_DIM:]], axis=0)

    def finalize(q):
        acc = acc_ref[...]
        o = acc[:DIFF_V_DIM, :] / acc[DIFF_V_DIM:DIFF_V_DIM + 1, :]
        lam = (jnp.exp(jnp.sum(lq1_ref[...] * lk1_ref[...], axis=-1, keepdims=True))
               - jnp.exp(jnp.sum(lq2_ref[...] * lk2_ref[...], axis=-1, keepdims=True)) + LAMBDA_INIT)
        y = o[:, :blk] - lam * o[:, blk:]
        y = y * lax.rsqrt(jnp.mean(y * y, axis=0, keepdims=True) + EPS) * sg_ref[...]
        y = y * (1.0 - LAMBDA_INIT)
        for c in range(per_block):
            o_ref[q * per_block + c] = y[:, c * chunk:(c + 1) * chunk].astype(o_ref.dtype)

    def mask_fn(key_pos, query_pos):
        return key_pos // CHUNK <= query_pos // CHUNK

    _attn_head(load_rhs, finalize, mask_fn, k_ref, vT_ref, rhs_ref, s_ref, mblk_ref, m_ref, acc_ref,
               blk=blk, n_maps=2, q_tile=DIFF_Q_TILE, tiles_abreast=1, trip_sizes=(4, 2))


def _head_spec(arr, rows):
    return pl.BlockSpec((arr.shape[0], rows, arr.shape[2]), lambda h: (0, h, 0))


def _attn_scratch(blk, n_maps, v_rows):
    nq = n_maps * blk
    return [pltpu.VMEM((LANES, nq), BF16), pltpu.VMEM((blk, nq), F32),
            pltpu.VMEM((1, nq), F32), pltpu.VMEM((1, nq), F32), pltpu.VMEM((v_rows, nq), F32)]


def _fox_attention(fqT, fk, fvT):
    seq = fk.shape[0]
    out_shape = jax.ShapeDtypeStruct((fqT.shape[0], WIDTH, fqT.shape[2]), BF16)
    return pl.pallas_call(
        _fox_kernel,
        grid=(N_HEADS_FOX,),
        in_specs=[
            _head_spec(fqT, LANES),
            pl.BlockSpec((seq, LANES), lambda h: (0, h)),
            _head_spec(fvT, FOX_V_ROWS),
        ],
        out_specs=_head_spec(out_shape, HEAD_DIM),
        out_shape=out_shape,
        scratch_shapes=_attn_scratch(FOX_BLOCK, 1, FOX_V_ROWS),
        compiler_params=pltpu.CompilerParams(
            dimension_semantics=("arbitrary",), vmem_limit_bytes=VMEM_LIMIT_BYTES),
        name="fox_attn",
    )(fqT, fk, fvT)


def _diff_attention(dqT, dk, dvT, lq1, lk1, lq2, lk2, sg):
    seq = dk.shape[0]
    small = lambda a: pl.BlockSpec(a.shape, lambda h: (0, 0))
    out_shape = jax.ShapeDtypeStruct(dqT.shape, BF16)
    return pl.pallas_call(
        _diff_kernel,
        grid=(N_HEADS_DIFF,),
        in_specs=[
            _head_spec(dqT, 2 * HEAD_DIM),
            pl.BlockSpec((seq, LANES), lambda h: (0, h)),
            _head_spec(dvT, DIFF_V_ROWS),
            small(lq1), small(lk1), small(lq2), small(lk2), small(sg),
        ],
        out_specs=_head_spec(out_shape, DIFF_V_DIM),
        out_shape=out_shape,
        scratch_shapes=_attn_scratch(DIFF_BLOCK, 2, DIFF_V_ROWS),
        compiler_params=pltpu.CompilerParams(
            dimension_semantics=("arbitrary",), vmem_limit_bytes=VMEM_LIMIT_BYTES),
        name="diff_attn",
    )(dqT, dk, dvT, lq1, lk1, lq2, lk2, sg)


def _out_kernel(yf_ref, yd_ref, gf_ref, gd_ref, x_ref, w_ref, g_ref, o_ref):
    tn = (((0,), (0,)), ((), ()))
    n_chunks, _, tm = yf_ref.shape
    for c in range(n_chunks):
        gated = lambda y_ref, gate_ref: (y_ref[c].astype(F32) * gate_ref[c].astype(F32)).astype(BF16)
        y = lax.dot_general(gated(yf_ref, gf_ref), w_ref[:WIDTH, :], tn, preferred_element_type=F32)
        y = y + lax.dot_general(gated(yd_ref, gd_ref), w_ref[WIDTH:, :], tn, preferred_element_type=F32)
        h = x_ref[c * tm:(c + 1) * tm, :] + y
        o_ref[c * tm:(c + 1) * tm, :] = h * lax.rsqrt(jnp.mean(h * h, axis=-1, keepdims=True) + EPS) * g_ref[...]


def _out_project(yT_fox, yT_diff, fgT, dgT, x2d, w_out, final_g):
    seq = x2d.shape[0]
    nblk, _, tm = yT_fox.shape
    per_step = OUT_ROW_BLOCK // tm
    chunk = pl.BlockSpec((per_step, WIDTH, tm), lambda i: (i, 0, 0))
    return pl.pallas_call(
        _out_kernel,
        grid=(nblk // per_step,),
        in_specs=[
            chunk, chunk, chunk, chunk,
            pl.BlockSpec((OUT_ROW_BLOCK, D_MODEL), lambda i: (i, 0)),
            pl.BlockSpec(w_out.shape, lambda i: (0, 0)),
            pl.BlockSpec((1, D_MODEL), lambda i: (0, 0)),
        ],
        out_specs=pl.BlockSpec((OUT_ROW_BLOCK, D_MODEL), lambda i: (i, 0)),
        out_shape=jax.ShapeDtypeStruct((seq, D_MODEL), F32),
        compiler_params=pltpu.CompilerParams(
            dimension_semantics=("arbitrary",), vmem_limit_bytes=VMEM_LIMIT_BYTES),
        name="out_proj",
    )(yT_fox, yT_diff, fgT, dgT, x2d, w_out, final_g)


_COL_SIZES = (("fq", WIDTH), ("fk", WIDTH), ("fv", WIDTH), ("fg", WIDTH), ("fz", N_HEADS_FOX),
              ("dq", WIDTH), ("dk", WIDTH), ("dv", WIDTH), ("dg", WIDTH))
_COL_OFF = {name: sum(size for _, size in _COL_SIZES[:j]) for j, (name, _) in enumerate(_COL_SIZES)}


def _prep_kernel(w_ref, place_ref, wt_ref, wfk_ref):
    d = w_ref.shape[2]

    def group(name, rows=WIDTH):
        off = _COL_OFF[name]
        return w_ref[0, off:off + rows, :]

    for gi, name in enumerate(_T_GROUPS):
        wt_ref[gi * WIDTH:(gi + 1) * WIDTH, :] = group(name).astype(BF16)
    fz = jnp.concatenate([group("fz", N_HEADS_FOX), jnp.zeros((GATE_ROWS - N_HEADS_FOX, d), F32)], axis=0)
    wt_ref[_T_OFF["fz"]:_T_ROWS, :] = fz.astype(BF16)
    wfk_ref[...] = _dot(group("fk").T.astype(BF16), place_ref[...]).astype(BF16)


def _prepare_weights(w):
    _, n_cols, d = w.shape
    place = np.zeros((WIDTH, N_HEADS_FOX * LANES), np.float32)
    for h in range(N_HEADS_FOX):
        place[h * HEAD_DIM + np.arange(HEAD_DIM), h * LANES + np.arange(HEAD_DIM)] = 1.0
    place = jnp.asarray(place, BF16)
    full = lambda shape: pl.BlockSpec(shape, lambda i: (0,) * len(shape))
    out_shape = (jax.ShapeDtypeStruct((_T_ROWS, d), BF16),
                 jax.ShapeDtypeStruct((d, N_HEADS_FOX * LANES), BF16))
    return pl.pallas_call(
        _prep_kernel,
        grid=(1,),
        in_specs=[pl.BlockSpec((1, n_cols, d), lambda i: (0, 0, 0), pipeline_mode=pl.Buffered(1)), full(place.shape)],
        out_specs=tuple(full(o.shape) for o in out_shape),
        out_shape=out_shape,
        compiler_params=pltpu.CompilerParams(
            dimension_semantics=("arbitrary",), vmem_limit_bytes=VMEM_LIMIT_BYTES),
        name="prep_weights",
    )(w, place)


def _bias_selector():
    sel = np.zeros((LANES, N_HEADS_FOX * LANES), np.float32)
    for p in range(N_BIAS_PIECES):
        for h in range(N_HEADS_FOX):
            sel[p * GATE_ROWS + h, h * LANES + HEAD_DIM + p] = 1.0
    return jnp.asarray(sel, BF16)


def kernel(x, norm_g, w_in, b_forget, lambda_q1, lambda_k1, lambda_q2, lambda_k2, subln_g, w_out, final_g):
    batch, seq, _ = x.shape
    assert batch == 1
    for blk in (FOX_BLOCK, DIFF_BLOCK):
        assert seq % blk == 0 and seq // blk >= 3 and blk % ROW_BLOCK == 0
    x2d = x[0]
    wt, wfk = _prepare_weights(jnp.swapaxes(w_in, 1, 2))
    bf = jnp.pad(b_forget[0].astype(F32), (0, GATE_ROWS - N_HEADS_FOX)).reshape(GATE_ROWS, 1)
    invf = (ROPE_THETA ** (-jnp.arange(0, HEAD_DIM, 2, dtype=F32) / HEAD_DIM)).reshape(HEAD_DIM // 2, 1)

    fqT, fk, fvT, fgT, dqT, dk, dvT, dgT = _project(
        x2d, norm_g[0].reshape(1, D_MODEL), wt, wfk, bf, invf, _bias_selector())

    yT_fox = _fox_attention(fqT, fk, fvT)
    row = lambda a: a[0].astype(F32).reshape(1, HEAD_DIM)
    yT_diff = _diff_attention(dqT, dk, dvT, row(lambda_q1), row(lambda_k1), row(lambda_q2), row(lambda_k2),
                              subln_g[0].astype(F32).reshape(DIFF_V_DIM, 1))
    out = _out_project(yT_fox, yT_diff, fgT, dgT, x2d, w_out[0].astype(BF16), final_g.reshape(1, D_MODEL))
    return out[None]
```

```python
import math

import jax
import jax.numpy as jnp
import numpy as np
from jax import lax
from jax.experimental import pallas as pl
from jax.experimental.pallas import tpu as pltpu

D_MODEL = 1024
HEAD_DIM = 64
N_HEADS_FOX = 8
N_HEADS_DIFF = 4
N_MAPS_DIFF = 2 * N_HEADS_DIFF
DIFF_V_DIM = 2 * HEAD_DIM
WIDTH = 512
CHUNK = 64
ROPE_THETA = 10000.0
EPS = 1e-6
LAMBDA_INIT = 0.8 - 0.6 * math.exp(-0.3 * 0)
SCALE = HEAD_DIM ** -0.5
LOG2E = math.log2(math.e)
Q_SCALE = SCALE * LOG2E

LANES = 128
ROW_BLOCK = 512
OUT_ROW_BLOCK = 1024
FOX_BLOCK = 2048
DIFF_BLOCK = 1024
SUB_KEYS = 256
FOX_Q_TILE = 512
DIFF_Q_TILE = 256
N_BIAS_PIECES = 3
BF16_SUBLANES = 16
GATE_ROWS = BF16_SUBLANES
FOX_V_ROWS = HEAD_DIM + BF16_SUBLANES
DIFF_V_ROWS = DIFF_V_DIM + BF16_SUBLANES
VMEM_LIMIT_BYTES = 56 * 1024 * 1024

F32 = jnp.float32
BF16 = jnp.bfloat16

_T_GROUPS = ("fq", "fv", "fg", "dq", "dk", "dv", "dg")
_T_OFF = {name: i * WIDTH for i, name in enumerate(_T_GROUPS + ("fz",))}
_T_ROWS = _T_OFF["fz"] + GATE_ROWS


def _dot_nt(a, b):
    return lax.dot_general(a, b, (((1,), (1,)), ((), ())), preferred_element_type=F32)


def _dot(a, b):
    return jnp.dot(a, b, preferred_element_type=F32)


def _split_bf16(x):
    hi = x.astype(BF16)
    r = x - hi.astype(F32)
    mid = r.astype(BF16)
    lo = (r - mid.astype(F32)).astype(BF16)
    return hi, mid, lo


def _proj_kernel(x_ref, ng_ref, wt_ref, wfk_ref, bf_ref, invf_ref, sel_ref,
                 fqT_ref, fk_ref, fvT_ref, fgT_ref, dqT_ref, dk_ref, dvT_ref, dgT_ref,
                 carry_ref):
    i = pl.program_id(0)
    tm = x_ref.shape[0]

    @pl.when(i == 0)
    def _():
        carry_ref[...] = jnp.zeros_like(carry_ref)

    x = x_ref[...]
    u = x * lax.rsqrt(jnp.mean(x * x, axis=-1, keepdims=True) + EPS) * ng_ref[...]
    ub = u.astype(BF16)

    def proj_t(name, width=WIDTH):
        off = _T_OFF[name]
        return _dot_nt(wt_ref[off:off + width, :], ub)


    zT = proj_t("fz", GATE_ROWS) + bf_ref[...]
    logf = jax.nn.log_sigmoid(zT)
    lane = lax.broadcasted_iota(jnp.int32, logf.shape, 1)
    shift = 1
    while shift < tm:
        logf = logf + jnp.where(lane >= shift, pltpu.roll(logf, shift, axis=1), 0.0)
        shift *= 2
    cum = logf + carry_ref[...]
    carry_ref[...] = cum[:, tm - 1:tm]
    pieces = [p.astype(F32) for p in _split_bf16(cum * (-LOG2E))]
    pieces.append(jnp.zeros((LANES - N_BIAS_PIECES * GATE_ROWS, tm), F32))
    pieces_t = jnp.concatenate(pieces, axis=0).T.astype(BF16)

    fg = proj_t("fg")
    fgT_ref[0] = (fg * jax.nn.sigmoid(fg)).astype(BF16)
    dg = proj_t("dg")
    dgT_ref[0] = (dg * jax.nn.sigmoid(dg)).astype(BF16)

    pos = (i * tm + lax.broadcasted_iota(jnp.int32, (1, tm), 1)).astype(F32)
    ang = pos * invf_ref[...]
    cos, sin = jnp.cos(ang), jnp.sin(ang)
    half = HEAD_DIM // 2

    def rope_t(xt):
        x1, x2 = xt[:half, :], xt[half:, :]
        return jnp.concatenate([x1 * cos - x2 * sin, x2 * cos + x1 * sin], axis=0)

    qT = proj_t("dq")
    for m in range(N_MAPS_DIFF):
        rows = slice(m * HEAD_DIM, (m + 1) * HEAD_DIM)
        dqT_ref[0, rows, :] = (rope_t(qT[rows, :]) * Q_SCALE).astype(BF16)
    kT = proj_t("dk")
    kT = jnp.concatenate([rope_t(kT[m * HEAD_DIM:(m + 1) * HEAD_DIM, :]) for m in range(N_MAPS_DIFF)], axis=0)
    dk_ref[...] = kT.T.astype(BF16)

    qT = proj_t("fq") * Q_SCALE
    row = lax.broadcasted_iota(jnp.int32, (LANES - HEAD_DIM, tm), 0)
    ones_rows = jnp.where(row < N_BIAS_PIECES, 1.0, 0.0).astype(BF16)
    for h in range(N_HEADS_FOX):
        fqT_ref[0, h * LANES:h * LANES + HEAD_DIM, :] = qT[h * HEAD_DIM:(h + 1) * HEAD_DIM, :].astype(BF16)
        fqT_ref[0, h * LANES + HEAD_DIM:(h + 1) * LANES, :] = ones_rows

    row = lax.broadcasted_iota(jnp.int32, (BF16_SUBLANES, tm), 0)
    denom_rows = jnp.where(row == 0, 1.0, 0.0).astype(BF16)
    vT = proj_t("fv").astype(BF16)
    for h in range(N_HEADS_FOX):
        fvT_ref[0, h * FOX_V_ROWS:h * FOX_V_ROWS + HEAD_DIM, :] = vT[h * HEAD_DIM:(h + 1) * HEAD_DIM, :]
        fvT_ref[0, h * FOX_V_ROWS + HEAD_DIM:(h + 1) * FOX_V_ROWS, :] = denom_rows
    vT = proj_t("dv").astype(BF16)
    for h in range(N_HEADS_DIFF):
        dvT_ref[0, h * DIFF_V_ROWS:h * DIFF_V_ROWS + DIFF_V_DIM, :] = vT[h * DIFF_V_DIM:(h + 1) * DIFF_V_DIM, :]
        dvT_ref[0, h * DIFF_V_ROWS + DIFF_V_DIM:(h + 1) * DIFF_V_ROWS, :] = denom_rows

    fk_ref[...] = (_dot(ub, wfk_ref[...]) + _dot(pieces_t, sel_ref[...])).astype(BF16)


def _project(x2d, norm_g, wt, wfk, bf, invf, sel):
    seq = x2d.shape[0]
    tm = ROW_BLOCK
    nblk = seq // tm
    const = lambda shape: pl.BlockSpec(shape, lambda i: (0,) * len(shape))
    chunk_t = lambda rows: pl.BlockSpec((1, rows, tm), lambda i: (i, 0, 0))
    chunked = lambda rows: jax.ShapeDtypeStruct((nblk, rows, tm), BF16)
    out_shape = (
        chunked(N_HEADS_FOX * LANES),
        jax.ShapeDtypeStruct((seq, N_HEADS_FOX * LANES), BF16),
        chunked(N_HEADS_FOX * FOX_V_ROWS),
        chunked(WIDTH),
        chunked(WIDTH),
        jax.ShapeDtypeStruct((seq, WIDTH), BF16),
        chunked(N_HEADS_DIFF * DIFF_V_ROWS),
        chunked(WIDTH),
    )
    return pl.pallas_call(
        _proj_kernel,
        grid=(nblk,),
        in_specs=[
            pl.BlockSpec((tm, D_MODEL), lambda i: (i, 0)),
            const((1, D_MODEL)),
            const(wt.shape), const(wfk.shape), const(bf.shape),
            const(invf.shape), const(sel.shape),
        ],
        out_specs=(
            chunk_t(N_HEADS_FOX * LANES),
            pl.BlockSpec((tm, N_HEADS_FOX * LANES), lambda i: (i, 0)),
            chunk_t(N_HEADS_FOX * FOX_V_ROWS), chunk_t(WIDTH),
            chunk_t(WIDTH),
            pl.BlockSpec((tm, WIDTH), lambda i: (i, 0)),
            chunk_t(N_HEADS_DIFF * DIFF_V_ROWS), chunk_t(WIDTH),
        ),
        out_shape=out_shape,
        scratch_shapes=[pltpu.VMEM((GATE_ROWS, 1), F32)],
        compiler_params=pltpu.CompilerParams(
            dimension_semantics=("arbitrary",), vmem_limit_bytes=VMEM_LIMIT_BYTES),
        name="proj",
    )(x2d, norm_g, wt, wfk, bf, invf, sel)


def _attn_head(load_rhs, finalize, mask_fn, k_ref, vT_ref, rhs_ref, s_ref, mblk_ref, m_ref, acc_ref,
               *, blk, n_maps, q_tile, tiles_abreast, steps_per_trip):
    nq = n_maps * blk
    v_rows, v_chunk = vT_ref.shape[1:]
    n_sub = blk // SUB_KEYS
    n_qblocks = k_ref.shape[0] // blk
    neg_inf = lambda cols: jnp.full((1, cols), -jnp.inf, F32)

    def tile_kind(diag, k0, q0):
        if not diag or k0 + SUB_KEYS <= q0:
            return "full"
        return "skip" if k0 >= q0 + q_tile else "mask"

    def step(consume, produce):
        if consume is not None:
            m_old = m_ref[...]
            m_new = jnp.maximum(m_old, mblk_ref[...])
            m_ref[...] = m_new
            alpha = jnp.exp2(m_old - m_new)
        n_cq = nq // q_tile
        for cq0 in range(0, n_cq, tiles_abreast):
            group = range(cq0, min(cq0 + tiles_abreast, n_cq))
            cols = {cq: slice(cq * q_tile, (cq + 1) * q_tile) for cq in group}
            q0 = {cq: (cq * q_tile) % blk for cq in group}
            pv = {cq: None for cq in group}
            m_next = {cq: None for cq in group}
            for j in range(n_sub):
                k0 = j * SUB_KEYS
                rows = slice(k0, k0 + SUB_KEYS)
                p_kind = {cq: tile_kind(produce[1], k0, q0[cq]) if produce is not None else "skip" for cq in group}
                c_kind = {cq: tile_kind(consume[1], k0, q0[cq]) if consume is not None else "skip" for cq in group}
                s = {}
                for cq in group:
                    if p_kind[cq] == "skip":
                        continue
                    row0 = produce[0] * blk + k0
                    if not isinstance(row0, int):
                        row0 = pl.multiple_of(row0, SUB_KEYS)
                    s[cq] = _dot(k_ref[pl.ds(row0, SUB_KEYS), :], rhs_ref[:, cols[cq]])
                    if p_kind[cq] == "mask":
                        key_pos = k0 + lax.broadcasted_iota(jnp.int32, s[cq].shape, 0)
                        query_pos = q0[cq] + lax.broadcasted_iota(jnp.int32, s[cq].shape, 1)
                        s[cq] = jnp.where(mask_fn(key_pos, query_pos), s[cq], -jnp.inf)
                for cq in group:
                    if c_kind[cq] == "skip":
                        continue
                    c, off = divmod(k0, v_chunk)
                    v_sub = vT_ref[consume[0] * (blk // v_chunk) + c, :, off:off + SUB_KEYS]
                    d = _dot(v_sub, jnp.exp2(s_ref[rows, cols[cq]] - m_new[:, cols[cq]]).astype(BF16))
                    pv[cq] = d if pv[cq] is None else pv[cq] + d
                for cq in s:
                    s_ref[rows, cols[cq]] = s[cq]
                    m_j = jnp.max(s[cq], axis=0, keepdims=True)
                    m_next[cq] = m_j if m_next[cq] is None else jnp.maximum(m_next[cq], m_j)
            for cq in group:
                if consume is not None:
                    acc_ref[:, cols[cq]] = alpha[:, cols[cq]] * acc_ref[:, cols[cq]] + pv[cq]
                if produce is not None:
                    mblk_ref[:, cols[cq]] = m_next[cq]

    def reset():
        m_ref[...] = neg_inf(nq)
        acc_ref[...] = jnp.zeros_like(acc_ref)

    def key_blocks_1_to(q):
        def trip(i, _):
            for t in range(steps_per_trip):
                kb = 1 + steps_per_trip * i + t
                step((kb - 1, False), (kb, False))

        n_trips = (q - 1) // steps_per_trip
        lax.fori_loop(0, n_trips, trip, None)
        lax.fori_loop(1 + steps_per_trip * n_trips, q, lambda kb, _: step((kb - 1, False), (kb, False)), None)

    def next_query_block(q, consume):
        load_rhs(q)
        step(consume, (q, True))
        finalize(q - 1)
        reset()
        step((q, True), (0, False))

    reset()
    load_rhs(0)
    step(None, (0, True))
    next_query_block(1, (0, True))

    def middle(q, _):
        key_blocks_1_to(q)
        next_query_block(q + 1, (q - 1, False))

    lax.fori_loop(1, n_qblocks - 1, middle, None)
    last = n_qblocks - 1
    key_blocks_1_to(last)
    step((last - 1, False), None)
    finalize(last)


def _fox_kernel(qT_ref, k_ref, vT_ref, o_ref, rhs_ref, s_ref, mblk_ref, m_ref, acc_ref):
    blk = s_ref.shape[0]
    chunk = qT_ref.shape[2]
    per_block = blk // chunk

    def load_rhs(q):
        for c in range(per_block):
            rhs_ref[:, c * chunk:(c + 1) * chunk] = qT_ref[q * per_block + c]

    def finalize(q):
        acc = acc_ref[...]
        o = acc[:HEAD_DIM, :] / acc[HEAD_DIM:HEAD_DIM + 1, :]
        for c in range(per_block):
            o_ref[q * per_block + c] = o[:, c * chunk:(c + 1) * chunk].astype(o_ref.dtype)

    def mask_fn(key_pos, query_pos):
        return key_pos <= query_pos

    _attn_head(load_rhs, finalize, mask_fn, k_ref, vT_ref, rhs_ref, s_ref, mblk_ref, m_ref, acc_ref,
               blk=blk, n_maps=1, q_tile=FOX_Q_TILE, tiles_abreast=1, steps_per_trip=2)


def _diff_kernel(qT_ref, k_ref, vT_ref, lq1_ref, lk1_ref, lq2_ref, lk2_ref, sg_ref, o_ref,
                 rhs_ref, s_ref, mblk_ref, m_ref, acc_ref):
    blk = s_ref.shape[0]
    chunk = qT_ref.shape[2]
    per_block = blk // chunk

    def load_rhs(q):
        zeros = jnp.zeros((HEAD_DIM, chunk), BF16)
        for c in range(per_block):
            q12 = qT_ref[q * per_block + c]
            rhs_ref[:, c * chunk:(c + 1) * chunk] = jnp.concatenate([q12[:HEAD_DIM], zeros], axis=0)
            rhs_ref[:, blk + c * chunk:blk + (c + 1) * chunk] = jnp.concatenate([zeros, q12[HEAD_DIM:]], axis=0)

    def finalize(q):
        acc = acc_ref[...]
        o = acc[:DIFF_V_DIM, :] / acc[DIFF_V_DIM:DIFF_V_DIM + 1, :]
        lam = (jnp.exp(jnp.sum(lq1_ref[...] * lk1_ref[...], axis=-1, keepdims=True))
               - jnp.exp(jnp.sum(lq2_ref[...] * lk2_ref[...], axis=-1, keepdims=True)) + LAMBDA_INIT)
        y = o[:, :blk] - lam * o[:, blk:]
        y = y * lax.rsqrt(jnp.mean(y * y, axis=0, keepdims=True) + EPS) * sg_ref[...]
        y = y * (1.0 - LAMBDA_INIT)
        for c in range(per_block):
            o_ref[q * per_block + c] = y[:, c * chunk:(c + 1) * chunk].astype(o_ref.dtype)

    def mask_fn(key_pos, query_pos):
        return key_pos // CHUNK <= query_pos // CHUNK

    _attn_head(load_rhs, finalize, mask_fn, k_ref, vT_ref, rhs_ref, s_ref, mblk_ref, m_ref, acc_ref,
               blk=blk, n_maps=2, q_tile=DIFF_Q_TILE, tiles_abreast=1, steps_per_trip=4)


def _head_spec(arr, rows):
    return pl.BlockSpec((arr.shape[0], rows, arr.shape[2]), lambda h: (0, h, 0))


def _attn_scratch(blk, n_maps, v_rows):
    nq = n_maps * blk
    return [pltpu.VMEM((LANES, nq), BF16), pltpu.VMEM((blk, nq), F32),
            pltpu.VMEM((1, nq), F32), pltpu.VMEM((1, nq), F32), pltpu.VMEM((v_rows, nq), F32)]


def _fox_attention(fqT, fk, fvT):
    seq = fk.shape[0]
    out_shape = jax.ShapeDtypeStruct((fqT.shape[0], WIDTH, fqT.shape[2]), BF16)
    return pl.pallas_call(
        _fox_kernel,
        grid=(N_HEADS_FOX,),
        in_specs=[
            _head_spec(fqT, LANES),
            pl.BlockSpec((seq, LANES), lambda h: (0, h)),
            _head_spec(fvT, FOX_V_ROWS),
        ],
        out_specs=_head_spec(out_shape, HEAD_DIM),
        out_shape=out_shape,
        scratch_shapes=_attn_scratch(FOX_BLOCK, 1, FOX_V_ROWS),
        compiler_params=pltpu.CompilerParams(
            dimension_semantics=("arbitrary",), vmem_limit_bytes=VMEM_LIMIT_BYTES),
        name="fox_attn",
    )(fqT, fk, fvT)


def _diff_attention(dqT, dk, dvT, lq1, lk1, lq2, lk2, sg):
    seq = dk.shape[0]
    small = lambda a: pl.BlockSpec(a.shape, lambda h: (0, 0))
    out_shape = jax.ShapeDtypeStruct(dqT.shape, BF16)
    return pl.pallas_call(
        _diff_kernel,
        grid=(N_HEADS_DIFF,),
        in_specs=[
            _head_spec(dqT, 2 * HEAD_DIM),
            pl.BlockSpec((seq, LANES), lambda h: (0, h)),
            _head_spec(dvT, DIFF_V_ROWS),
            small(lq1), small(lk1), small(lq2), small(lk2), small(sg),
        ],
        out_specs=_head_spec(out_shape, DIFF_V_DIM),
        out_shape=out_shape,
        scratch_shapes=_attn_scratch(DIFF_BLOCK, 2, DIFF_V_ROWS),
        compiler_params=pltpu.CompilerParams(
            dimension_semantics=("arbitrary",), vmem_limit_bytes=VMEM_LIMIT_BYTES),
        name="diff_attn",
    )(dqT, dk, dvT, lq1, lk1, lq2, lk2, sg)


def _out_kernel(yf_ref, yd_ref, gf_ref, gd_ref, x_ref, w_ref, g_ref, o_ref):
    tn = (((0,), (0,)), ((), ()))
    n_chunks, _, tm = yf_ref.shape
    for c in range(n_chunks):
        gated = lambda y_ref, gate_ref: (y_ref[c].astype(F32) * gate_ref[c].astype(F32)).astype(BF16)
        y = lax.dot_general(gated(yf_ref, gf_ref), w_ref[:WIDTH, :], tn, preferred_element_type=F32)
        y = y + lax.dot_general(gated(yd_ref, gd_ref), w_ref[WIDTH:, :], tn, preferred_element_type=F32)
        h = x_ref[c * tm:(c + 1) * tm, :] + y
        o_ref[c * tm:(c + 1) * tm, :] = h * lax.rsqrt(jnp.mean(h * h, axis=-1, keepdims=True) + EPS) * g_ref[...]


def _out_project(yT_fox, yT_diff, fgT, dgT, x2d, w_out, final_g):
    seq = x2d.shape[0]
    nblk, _, tm = yT_fox.shape
    per_step = OUT_ROW_BLOCK // tm
    chunk = pl.BlockSpec((per_step, WIDTH, tm), lambda i: (i, 0, 0))
    return pl.pallas_call(
        _out_kernel,
        grid=(nblk // per_step,),
        in_specs=[
            chunk, chunk, chunk, chunk,
            pl.BlockSpec((OUT_ROW_BLOCK, D_MODEL), lambda i: (i, 0)),
            pl.BlockSpec(w_out.shape, lambda i: (0, 0)),
            pl.BlockSpec((1, D_MODEL), lambda i: (0, 0)),
        ],
        out_specs=pl.BlockSpec((OUT_ROW_BLOCK, D_MODEL), lambda i: (i, 0)),
        out_shape=jax.ShapeDtypeStruct((seq, D_MODEL), F32),
        compiler_params=pltpu.CompilerParams(
            dimension_semantics=("arbitrary",), vmem_limit_bytes=VMEM_LIMIT_BYTES),
        name="out_proj",
    )(yT_fox, yT_diff, fgT, dgT, x2d, w_out, final_g)


_COL_SIZES = (("fq", WIDTH), ("fk", WIDTH), ("fv", WIDTH), ("fg", WIDTH), ("fz", N_HEADS_FOX),
              ("dq", WIDTH), ("dk", WIDTH), ("dv", WIDTH), ("dg", WIDTH))
_COL_OFF = {name: sum(size for _, size in _COL_SIZES[:j]) for j, (name, _) in enumerate(_COL_SIZES)}


def _prep_kernel(w_ref, place_ref, wt_ref, wfk_ref):
    d = w_ref.shape[2]

    def group(name, rows=WIDTH):
        off = _COL_OFF[name]
        return w_ref[0, off:off + rows, :]

    for gi, name in enumerate(_T_GROUPS):
        wt_ref[gi * WIDTH:(gi + 1) * WIDTH, :] = group(name).astype(BF16)
    fz = jnp.concatenate([group("fz", N_HEADS_FOX), jnp.zeros((GATE_ROWS - N_HEADS_FOX, d), F32)], axis=0)
    wt_ref[_T_OFF["fz"]:_T_ROWS, :] = fz.astype(BF16)
    wfk_ref[...] = _dot(group("fk").T.astype(BF16), place_ref[...]).astype(BF16)


def _prepare_weights(w):
    _, n_cols, d = w.shape
    place = np.zeros((WIDTH, N_HEADS_FOX * LANES), np.float32)
    for h in range(N_HEADS_FOX):
        place[h * HEAD_DIM + np.arange(HEAD_DIM), h * LANES + np.arange(HEAD_DIM)] = 1.0
    place = jnp.asarray(place, BF16)
    full = lambda shape: pl.BlockSpec(shape, lambda i: (0,) * len(shape))
    out_shape = (jax.ShapeDtypeStruct((_T_ROWS, d), BF16),
                 jax.ShapeDtypeStruct((d, N_HEADS_FOX * LANES), BF16))
    return pl.pallas_call(
        _prep_kernel,
        grid=(1,),
        in_specs=[pl.BlockSpec((1, n_cols, d), lambda i: (0, 0, 0), pipeline_mode=pl.Buffered(1)), full(place.shape)],
        out_specs=tuple(full(o.shape) for o in out_shape),
        out_shape=out_shape,
        compiler_params=pltpu.CompilerParams(
            dimension_semantics=("arbitrary",), vmem_limit_bytes=VMEM_LIMIT_BYTES),
        name="prep_weights",
    )(w, place)


def _bias_selector():
    sel = np.zeros((LANES, N_HEADS_FOX * LANES), np.float32)
    for p in range(N_BIAS_PIECES):
        for h in range(N_HEADS_FOX):
            sel[p * GATE_ROWS + h, h * LANES + HEAD_DIM + p] = 1.0
    return jnp.asarray(sel, BF16)


def kernel(x, norm_g, w_in, b_forget, lambda_q1, lambda_k1, lambda_q2, lambda_k2, subln_g, w_out, final_g):
    batch, seq, _ = x.shape
    assert batch == 1
    for blk in (FOX_BLOCK, DIFF_BLOCK):
        assert seq % blk == 0 and seq // blk >= 3 and blk % ROW_BLOCK == 0
    x2d = x[0]
    wt, wfk = _prepare_weights(jnp.swapaxes(w_in, 1, 2))
    bf = jnp.pad(b_forget[0].astype(F32), (0, GATE_ROWS - N_HEADS_FOX)).reshape(GATE_ROWS, 1)
    invf = (ROPE_THETA ** (-jnp.arange(0, HEAD_DIM, 2, dtype=F32) / HEAD_DIM)).reshape(HEAD_DIM // 2, 1)

    fqT, fk, fvT, fgT, dqT, dk, dvT, dgT = _project(
        x2d, norm_g[0].reshape(1, D_MODEL), wt, wfk, bf, invf, _bias_selector())

    yT_fox = _fox_attention(fqT, fk, fvT)
    row = lambda a: a[0].astype(F32).reshape(1, HEAD_DIM)
    yT_diff = _diff_attention(dqT, dk, dvT, row(lambda_q1), row(lambda_k1), row(lambda_q2), row(lambda_k2),
                              subln_g[0].astype(F32).reshape(DIFF_V_DIM, 1))
    out = _out_project(yT_fox, yT_diff, fgT, dgT, x2d, w_out[0].astype(BF16), final_g.reshape(1, D_MODEL))
    return out[None]
```
